```python
import math
import jax
import jax.numpy as jnp
from jax import lax
import numpy as np

D_MODEL = 2048
BATCH = 4
SEQ = 4096
DEPTH = 2

CHUNK = 64
Q_BLOCK = 128
A_HEADS = 8
A_KV_HEADS = 2
A_HEAD_DIM = 128
IDX_HEADS = 16
IDX_DIM = 64
TOPK_MAX = 256
REL_BUCKETS = 32
REL_MAX_DIST = 128
R_HEADS = 4
R_HEAD_DIM = 256
ROPE_BASE = 10000.0
A_Q = A_HEADS * A_HEAD_DIM
A_KV = A_KV_HEADS * A_HEAD_DIM
IDX_Q = IDX_HEADS * IDX_DIM
R_W = R_HEADS * R_HEAD_DIM
PROJ_SIZES = (A_Q, A_KV, A_KV, IDX_Q, IDX_DIM, IDX_HEADS, R_W, R_W, R_W, R_W)
IN_WIDTH = sum(PROJ_SIZES)
MIX_OUT = A_Q + R_W
D_FF = 5632
N_EXPERTS = 8
TOP_K = 2
D_FF_EXPERT = 7168
N_DENSE = (DEPTH + 1) // 2
N_MOE = DEPTH // 2
EPS = 1e-6
GN_EPS = 1e-5

kernel_name = 'hybrid_dsa_retention_moe_block'


def rms_norm(x, gain):
    xf = x.astype(jnp.float32)
    y = xf * lax.rsqrt(jnp.mean(xf * xf, axis=-1, keepdims=True) + EPS)
    return (y * gain.astype(jnp.float32)).astype(x.dtype)


def t5_bucket(rel):
    half = REL_BUCKETS // 2
    max_exact = half // 2
    ret = jnp.where(rel > 0, half, 0)
    n = jnp.abs(rel)
    nf = jnp.maximum(n, 1).astype(jnp.float32)
    large = max_exact + (jnp.log(nf / max_exact) / math.log(REL_MAX_DIST / max_exact)
                         * (half - max_exact)).astype(jnp.int32)
    large = jnp.minimum(large, half - 1)
    return ret + jnp.where(n < max_exact, n, large)


def rotary(x, pos):
    half = x.shape[-1] // 2
    inv = ROPE_BASE ** (-jnp.arange(half, dtype=jnp.float32) / half)
    ang = pos[:, None] * inv[None, :]
    cos = jnp.cos(ang)[:, None, :]
    sin = jnp.sin(ang)[:, None, :]
    x1, x2 = x[..., :half], x[..., half:]
    return jnp.concatenate([x1 * cos - x2 * sin, x1 * sin + x2 * cos], axis=-1)


def dsa_attention(q, k, v, iq, ik, iw, rel_bias):
    B, S = q.shape[0], q.shape[1]
    n_top = min(TOPK_MAX, S // 4)
    nblk = S // Q_BLOCK
    grp = A_HEADS // A_KV_HEADS
    scale = A_HEAD_DIM ** -0.5
    key_chunk = jnp.arange(S) // CHUNK
    ikf = ik.astype(jnp.float32)

    def to_blocks(a):
        return a.reshape((B, nblk, Q_BLOCK) + a.shape[2:]).swapaxes(0, 1)

    def one_block(args):
        qb, iqb, iwb, blk = args
        t = blk * Q_BLOCK + jnp.arange(Q_BLOCK)
        t_chunk = t // CHUNK
        admissible = key_chunk[None, :] <= t_chunk[:, None]
        hs = jax.nn.relu(jnp.einsum('bthd,bsd->bths', iqb.astype(jnp.float32), ikf))
        score = jnp.einsum('bths,bth->bts', hs, iwb.astype(jnp.float32))
        score = jnp.where(admissible[None], score, -jnp.inf)
        _, idx = lax.top_k(score, n_top)
        valid = (idx // CHUNK) <= t_chunk[None, :, None]
        k_sel = jax.vmap(lambda kb, ib: kb[ib])(k, idx)
        v_sel = jax.vmap(lambda vb, ib: vb[ib])(v, idx)
        qg = qb.reshape(B, Q_BLOCK, A_KV_HEADS, grp, A_HEAD_DIM)
        logits = jnp.einsum('btkgd,btjkd->btkgj', qg, k_sel).astype(jnp.float32) * scale
        bias = rel_bias[t5_bucket(idx - t[None, :, None])]
        bias = bias.reshape(B, Q_BLOCK, n_top, A_KV_HEADS, grp).transpose(0, 1, 3, 4, 2)
        logits = jnp.where(valid[:, :, None, None, :], logits + bias.astype(jnp.float32), -jnp.inf)
        p = jax.nn.softmax(logits, axis=-1).astype(v.dtype)
        o = jnp.einsum('btkgj,btjkd->btkgd', p, v_sel)
        return o.reshape(B, Q_BLOCK, A_Q)

    out = lax.map(one_block, (to_blocks(q), to_blocks(iq), to_blocks(iw), jnp.arange(nblk)))
    return out.swapaxes(0, 1).reshape(B, S, A_Q)


def retention(q, k, v, g, gn_gain):
    B, S = q.shape[0], q.shape[1]
    n = S // CHUNK
    pos = jnp.arange(S, dtype=jnp.float32)
    qf = rotary(q.astype(jnp.float32), pos)
    kf = rotary(k.astype(jnp.float32), pos) * (R_HEAD_DIM ** -0.5)
    vf = v.astype(jnp.float32)
    log_g = jnp.log(1.0 - jnp.exp2(-5.0 - jnp.arange(R_HEADS, dtype=jnp.float32)))
    c = jnp.arange(CHUNK, dtype=jnp.float32)
    diff = c[:, None] - c[None, :]
    dmask = jnp.where(diff >= 0, jnp.exp(log_g[:, None, None] * jnp.maximum(diff, 0.0)), 0.0)
    xi = jnp.exp(log_g[:, None] * (c[None, :] + 1.0))
    zeta = jnp.exp(log_g[:, None] * (CHUNK - 1.0 - c[None, :]))
    g_chunk = jnp.exp(log_g * CHUNK)

    def chunks(a):
        return a.reshape(B, n, CHUNK, R_HEADS, R_HEAD_DIM).transpose(1, 0, 3, 2, 4)

    def step(state, inp):
        qc, kc, vc = inp
        att = jnp.einsum('bhid,bhjd->bhij', qc, kc) * dmask[None]
        o = (jnp.einsum('bhij,bhjd->bhid', att, vc)
             + jnp.einsum('bhid,bhde->bhie', qc, state) * xi[None, :, :, None])
        state = (state * g_chunk[None, :, None, None]
                 + jnp.einsum('bhjd,bhje->bhde', kc * zeta[None, :, :, None], vc))
        return state, o

    state0 = jnp.zeros((B, R_HEADS, R_HEAD_DIM, R_HEAD_DIM), jnp.float32)
    _, o = lax.scan(step, state0, (chunks(qf), chunks(kf), chunks(vf)))
    o = o.transpose(1, 0, 3, 2, 4).reshape(B, S, R_HEADS, R_HEAD_DIM)
    mu = jnp.mean(o, axis=-1, keepdims=True)
    var = jnp.mean(jnp.square(o - mu), axis=-1, keepdims=True)
    o = ((o - mu) * lax.rsqrt(var + GN_EPS)).reshape(B, S, R_W) * gn_gain.astype(jnp.float32)
    return (jax.nn.silu(g.astype(jnp.float32)) * o).astype(g.dtype)


def mixer(h, w_in, q_gain, k_gain, gn_gain, w_out, rel_bias):
    B, S = h.shape[0], h.shape[1]
    proj = h @ w_in
    cuts = [int(s) for s in np.cumsum(PROJ_SIZES)[:-1]]
    q, k, v, iq, ik, iw, rq, rk, rv, rg = jnp.split(proj, cuts, axis=-1)
    q = rms_norm(q.reshape(B, S, A_HEADS, A_HEAD_DIM), q_gain)
    k = rms_norm(k.reshape(B, S, A_KV_HEADS, A_HEAD_DIM), k_gain)
    v = v.reshape(B, S, A_KV_HEADS, A_HEAD_DIM)
    iq = iq.reshape(B, S, IDX_HEADS, IDX_DIM)
    a_out = dsa_attention(q, k, v, iq, ik, iw, rel_bias)
    r_out = retention(rq.reshape(B, S, R_HEADS, R_HEAD_DIM), rk.reshape(B, S, R_HEADS, R_HEAD_DIM),
                      rv.reshape(B, S, R_HEADS, R_HEAD_DIM), rg, gn_gain)
    return jnp.concatenate([a_out, r_out], axis=-1) @ w_out


def swiglu(h, w_gate, w_up, w_down):
    return (jax.nn.silu(h @ w_gate) * (h @ w_up)) @ w_down


def moe_swiglu(h, w_router, w_gate, w_up, w_down):
    B, S, D = h.shape
    xt = h.reshape(B * S, D)
    logits = (xt @ w_router).astype(jnp.float32)
    top_v, top_i = lax.top_k(logits, TOP_K)
    gk = jax.nn.softmax(top_v, axis=-1)
    gates = jnp.sum(jax.nn.one_hot(top_i, N_EXPERTS, dtype=jnp.float32) * gk[..., None], axis=1)
    out = jnp.zeros_like(xt)
    for e in range(N_EXPERTS):
        ye = swiglu(xt, w_gate[e], w_up[e], w_down[e])
        out = out + gates[:, e:e + 1].astype(xt.dtype) * ye
    return out.reshape(B, S, D)


def setup_inputs(seed: int = 0) -> dict:
    key = jax.random.key(seed)
    ks = jax.random.split(key, 16)
    f32 = jnp.float32

    def w(k, shape, fan_in):
        return jax.random.normal(k, shape, f32) * (fan_in ** -0.5)

    def gain(k, shape):
        return 1.0 + 0.02 * jax.random.normal(k, shape, f32)

    return {
        'x': jax.random.normal(ks[0], (BATCH, SEQ, D_MODEL), f32),
        'rel_bias': 0.5 * jax.random.normal(ks[1], (REL_BUCKETS, A_HEADS), f32),
        'norm_mix': gain(ks[2], (DEPTH, D_MODEL)),
        'w_in': w(ks[3], (DEPTH, D_MODEL, IN_WIDTH), D_MODEL),
        'q_gain': gain(ks[4], (DEPTH, A_HEAD_DIM)),
        'k_gain': gain(ks[5], (DEPTH, A_HEAD_DIM)),
        'ret_gain': gain(ks[6], (DEPTH, R_W)),
        'w_out': w(ks[7], (DEPTH, MIX_OUT, D_MODEL), MIX_OUT),
        'norm_ffn': gain(ks[8], (DEPTH, D_MODEL)),
        'ffn_w_gate': w(ks[9], (N_DENSE, D_MODEL, D_FF), D_MODEL),
        'ffn_w_up': w(ks[10], (N_DENSE, D_MODEL, D_FF), D_MODEL),
        'ffn_w_down': w(ks[11], (N_DENSE, D_FF, D_MODEL), D_FF),
        'moe_router': w(ks[12], (N_MOE, D_MODEL, N_EXPERTS), D_MODEL),
        'moe_w_gate': w(ks[13], (N_MOE, N_EXPERTS, D_MODEL, D_FF_EXPERT), D_MODEL),
        'moe_w_up': w(ks[14], (N_MOE, N_EXPERTS, D_MODEL, D_FF_EXPERT), D_MODEL),
        'moe_w_down': w(ks[15], (N_MOE, N_EXPERTS, D_FF_EXPERT, D_MODEL), D_FF_EXPERT),
    }


def reference(x, rel_bias, norm_mix, w_in, q_gain, k_gain, ret_gain, w_out, norm_ffn,
              ffn_w_gate, ffn_w_up, ffn_w_down, moe_router, moe_w_gate, moe_w_up, moe_w_down):
    h = x
    for l in range(DEPTH):
        h = h + mixer(rms_norm(h, norm_mix[l]), w_in[l], q_gain[l], k_gain[l], ret_gain[l],
                      w_out[l], rel_bias)
        hn = rms_norm(h, norm_ffn[l])
        j = l // 2
        if l % 2 == 0:
            h = h + swiglu(hn, ffn_w_gate[j], ffn_w_up[j], ffn_w_down[j])
        else:
            h = h + moe_swiglu(hn, moe_router[j], moe_w_gate[j], moe_w_up[j], moe_w_down[j])
    return h
```

```python
import functools
import math

import jax
import jax.numpy as jnp
import numpy as np
from jax import lax
from jax.experimental import pallas as pl
from jax.experimental.pallas import tpu as pltpu

CHUNK = 64
A_HEADS = 8
A_KV_HEADS = 2
A_HEAD_DIM = 128
IDX_HEADS = 16
IDX_DIM = 64
TOPK_MAX = 256
REL_BUCKETS = 32
REL_MAX_DIST = 128
R_HEADS = 4
R_HEAD_DIM = 256
ROPE_BASE = 10000.0
A_Q = A_HEADS * A_HEAD_DIM
A_KV = A_KV_HEADS * A_HEAD_DIM
IDX_Q = IDX_HEADS * IDX_DIM
R_W = R_HEADS * R_HEAD_DIM
N_EXPERTS = 8
EPS = 1e-6
GN_EPS = 1e-5

LANES = 128
VMEM_LIMIT_BYTES = 56 * 1024 * 1024
MXU_DTYPE = jnp.bfloat16
DSA_TQ = 256
RET_CHUNK = 256
INT_MIN = -2147483648
MASKED = -1e30

_NT = (((1,), (1,)), ((), ()))
_TN = (((0,), (0,)), ((), ()))


def _cparams(*sem):
    return pltpu.CompilerParams(dimension_semantics=sem, vmem_limit_bytes=VMEM_LIMIT_BYTES)


def _dot(a, b):
    return jnp.dot(a, b, preferred_element_type=jnp.float32)


def _norm_kernel(x_ref, g_ref, o_ref):
    x = x_ref[...]
    ms = jnp.mean(x * x, axis=-1, keepdims=True)
    o_ref[...] = (x * lax.rsqrt(ms + EPS) * g_ref[...]).astype(o_ref.dtype)


def _norm_router_kernel(x_ref, g_ref, wr_ref, o_ref, gates_ref):
    x = x_ref[...]
    ms = jnp.mean(x * x, axis=-1, keepdims=True)
    xn = x * lax.rsqrt(ms + EPS) * g_ref[...]
    o_ref[...] = xn.astype(o_ref.dtype)
    x_hi = xn.astype(jnp.bfloat16)
    x_lo = (xn - x_hi.astype(jnp.float32)).astype(jnp.bfloat16)
    w = wr_ref[...]
    w_hi = w.astype(jnp.bfloat16)
    w_lo = (w - w_hi.astype(jnp.float32)).astype(jnp.bfloat16)
    logits = _dot(x_hi, w_hi) + (_dot(x_lo, w_hi) + _dot(x_hi, w_lo))
    lane = lax.broadcasted_iota(jnp.int32, logits.shape, 1)
    neg = jnp.float32(-jnp.inf)
    logits = jnp.where(lane < N_EXPERTS, logits, neg)
    m1 = jnp.max(logits, axis=-1, keepdims=True)
    i1 = jnp.min(jnp.where(logits == m1, lane, LANES), axis=-1, keepdims=True)
    rest = jnp.where(lane == i1, neg, logits)
    m2 = jnp.max(rest, axis=-1, keepdims=True)
    i2 = jnp.min(jnp.where(rest == m2, lane, LANES), axis=-1, keepdims=True)
    e2 = jnp.exp(m2 - m1)
    g1 = 1.0 / (1.0 + e2)
    g2 = e2 / (1.0 + e2)
    gates_ref[...] = jnp.where(lane == i1, g1, 0.0) + jnp.where(lane == i2, g2, 0.0)


def _rmsnorm(x, gain, router_w=None, tm=512):
    n, d = x.shape
    grid = (n // tm,)
    x_spec = pl.BlockSpec((tm, d), lambda i: (i, 0))
    g_spec = pl.BlockSpec((1, d), lambda i: (0, 0))
    o_spec = pl.BlockSpec((tm, d), lambda i: (i, 0))
    gain2 = gain.reshape(1, d)
    if router_w is None:
        return pl.pallas_call(
            _norm_kernel, grid=grid, in_specs=[x_spec, g_spec], out_specs=o_spec,
            out_shape=jax.ShapeDtypeStruct((n, d), MXU_DTYPE),
            compiler_params=_cparams("parallel"), name="rmsnorm")(x, gain2)
    wr = jnp.pad(router_w, ((0, 0), (0, LANES - router_w.shape[1])))
    return pl.pallas_call(
        _norm_router_kernel, grid=grid,
        in_specs=[x_spec, g_spec, pl.BlockSpec((d, LANES), lambda i: (0, 0))],
        out_specs=[o_spec, pl.BlockSpec((tm, LANES), lambda i: (i, 0))],
        out_shape=[jax.ShapeDtypeStruct((n, d), MXU_DTYPE),
                   jax.ShapeDtypeStruct((n, LANES), jnp.float32)],
        compiler_params=_cparams("parallel"), name="rmsnorm_router")(x, gain2, wr)


def _proj_kernel(x_ref, w_ref, gain_ref, o_ref, *, norm_heads, scale):
    acc = _dot(x_ref[...], w_ref[...])
    tn = acc.shape[1]
    for h in range(tn // A_HEAD_DIM):
        y = acc[:, h * A_HEAD_DIM:(h + 1) * A_HEAD_DIM]
        if h < norm_heads:
            ms = jnp.mean(y * y, axis=-1, keepdims=True)
            y = y * lax.rsqrt(ms + EPS) * (gain_ref[...] * scale)
        o_ref[:, h * A_HEAD_DIM:(h + 1) * A_HEAD_DIM] = y.astype(o_ref.dtype)


def _project(x, w, gain=None, *, norm_heads=0, scale=1.0, out_dtype=None, tm=1024, tn=512):
    out_dtype = out_dtype or MXU_DTYPE
    n, d = x.shape
    f = w.shape[1]
    tn = min(tn, f)
    if gain is None:
        gain = jnp.ones((A_HEAD_DIM,), jnp.float32)
    return pl.pallas_call(
        functools.partial(_proj_kernel, norm_heads=norm_heads, scale=scale),
        grid=(n // tm, f // tn),
        in_specs=[pl.BlockSpec((tm, d), lambda i, j: (i, 0)),
                  pl.BlockSpec((d, tn), lambda i, j: (0, j)),
                  pl.BlockSpec((1, A_HEAD_DIM), lambda i, j: (0, 0))],
        out_specs=pl.BlockSpec((tm, tn), lambda i, j: (i, j)),
        out_shape=jax.ShapeDtypeStruct((n, f), out_dtype),
        compiler_params=_cparams("parallel", "parallel"), name="in_proj")(
            x, w, gain.reshape(1, A_HEAD_DIM))


def _outproj_kernel(a_ref, r_ref, wa_ref, wr_ref, res_ref, o_ref):
    o_ref[...] = res_ref[...] + (_dot(a_ref[...], wa_ref[...]) + _dot(r_ref[...], wr_ref[...]))


def _out_project(a, r, w, res, tm=1024, tn=512):
    n, ka = a.shape
    kr = r.shape[1]
    assert ka == kr
    d = w.shape[1]
    tn = min(tn, d)
    return pl.pallas_call(
        _outproj_kernel, grid=(n // tm, d // tn),
        in_specs=[pl.BlockSpec((tm, ka), lambda i, j: (i, 0)),
                  pl.BlockSpec((tm, kr), lambda i, j: (i, 0)),
                  pl.BlockSpec((ka, tn), lambda i, j: (0, j)),
                  pl.BlockSpec((kr, tn), lambda i, j: (1, j)),
                  pl.BlockSpec((tm, tn), lambda i, j: (i, j))],
        out_specs=pl.BlockSpec((tm, tn), lambda i, j: (i, j)),
        out_shape=jax.ShapeDtypeStruct((n, d), jnp.float32),
        compiler_params=_cparams("parallel", "parallel"), name="out_proj")(a, r, w, w, res)


def _gu_kernel(x_ref, wg_ref, wu_ref, *rest, gated):
    o_ref = rest[-1]
    x = x_ref[...]
    g = _dot(x, wg_ref[0])
    u = _dot(x, wu_ref[0])
    y = g * jax.nn.sigmoid(g) * u
    if gated:
        gates = rest[0][...]
        lane = lax.broadcasted_iota(jnp.int32, gates.shape, 1)
        col = jnp.sum(jnp.where(lane == pl.program_id(0), gates, 0.0), axis=-1, keepdims=True)
        y = y * col
    o_ref[...] = y.astype(o_ref.dtype)


def _gate_up(x, wg, wu, gates=None, tm=1024, tn=512):
    n, d = x.shape
    e, _, f = wg.shape
    nj = f // tn
    in_specs = [pl.BlockSpec((tm, d), lambda ei, i, j: (i, 0)),
                pl.BlockSpec((1, d, tn), lambda ei, i, j: (ei, 0, j)),
                pl.BlockSpec((1, d, tn), lambda ei, i, j: (ei, 0, j))]
    args = [x, wg, wu]
    if gates is not None:
        in_specs.append(pl.BlockSpec((tm, LANES), lambda ei, i, j: (i, 0)))
        args.append(gates)
    return pl.pallas_call(
        functools.partial(_gu_kernel, gated=gates is not None),
        grid=(e, n // tm, nj), in_specs=in_specs,
        out_specs=pl.BlockSpec((tm, tn), lambda ei, i, j: (i, ei * nj + j)),
        out_shape=jax.ShapeDtypeStruct((n, e * f), MXU_DTYPE),
        compiler_params=_cparams("parallel", "parallel", "parallel"), name="gate_up")(*args)


def _down_kernel(x_ref, w_ref, res_ref, o_ref):
    @pl.when(pl.program_id(2) == 0)
    def _():
        o_ref[...] = res_ref[...]
    o_ref[...] += _dot(x_ref[...], w_ref[...])


def _down(x, w, res, tk, tm=512, tn=1024):
    n, kt = x.shape
    d = w.shape[1]
    return pl.pallas_call(
        _down_kernel, grid=(n // tm, d // tn, kt // tk),
        in_specs=[pl.BlockSpec((tm, tk), lambda i, j, k: (i, k)),
                  pl.BlockSpec((tk, tn), lambda i, j, k: (k, j)),
                  pl.BlockSpec((tm, tn), lambda i, j, k: (i, j))],
        out_specs=pl.BlockSpec((tm, tn), lambda i, j, k: (i, j)),
        out_shape=jax.ShapeDtypeStruct((n, d), jnp.float32),
        compiler_params=_cparams("parallel", "parallel", "arbitrary"), name="down_proj")(x, w, res)


def _sortable(x):
    x = jnp.where(x == 0.0, 0.0, x)
    i = lax.bitcast_convert_type(x, jnp.int32)
    return i ^ ((i >> 31) & jnp.int32(0x7FFFFFFF))


def _dsa_kernel(q_ref, kv_ref, iq_ref, ikk_ref, ikq_ref, nb_ref, o_ref,
                iklo_ref, ikhi_ref, wb_ref, skey_ref, tu_ref, jb_ref, m_ref, l_ref, acc_ref,
                *, n_top):
    tq = DSA_TQ
    half_rows = tq // 2
    qi = pl.program_id(1)
    f32 = jnp.float32

    @pl.when(qi == 0)
    def _():
        ik = ikk_ref[...]
        lane = lax.broadcasted_iota(jnp.int32, ik.shape, 1)
        iklo_ref[...] = jnp.where(lane < IDX_DIM, ik, 0.0).astype(MXU_DTYPE)
        shifted = pltpu.roll(ik, IDX_DIM, axis=1)
        ikhi_ref[...] = jnp.where(lane >= IDX_DIM, shifted, 0.0).astype(MXU_DTYPE)

    ikq = ikq_ref[...]
    for h in range(IDX_HEADS):
        wb_ref[h] = jnp.broadcast_to(ikq[:, IDX_DIM + h:IDX_DIM + h + 1], (tq, LANES))

    row_pos = qi * tq + lax.broadcasted_iota(jnp.int32, (half_rows, tq), 0)
    col_iota = lax.broadcasted_iota(jnp.int32, (half_rows, tq), 1)

    def score_block(kb, carry):
        r = pl.multiple_of(kb * tq, tq)
        klo = iklo_ref[pl.ds(r, tq), :]
        khi = ikhi_ref[pl.ds(r, tq), :]
        for half in range(2):
            rows = slice(half * half_rows, (half + 1) * half_rows)
            acc = jnp.zeros((half_rows, tq), f32)
            for p in range(IDX_HEADS // 2):
                lhs = iq_ref[rows, p * LANES:(p + 1) * LANES]
                for rhs, h in ((klo, 2 * p), (khi, 2 * p + 1)):
                    hs = lax.dot_general(lhs, rhs, _NT, preferred_element_type=f32)
                    w = wb_ref[h, rows, :]
                    acc = acc + jnp.maximum(hs, 0.0) * jnp.concatenate([w, w], axis=1)
            key = _sortable(acc)
            t_chunk = (row_pos + half * half_rows) // CHUNK
            s_chunk = (kb * tq + col_iota) // CHUNK
            skey_ref[kb, rows, :] = jnp.where(s_chunk <= t_chunk, key, INT_MIN)
        return carry

    lax.fori_loop(0, qi + 1, score_block, 0)

    halves = [slice(hf * half_rows, (hf + 1) * half_rows) for hf in range(2)]

    def count_where(make_pred):
        outs = []
        for rows in halves:
            pred = make_pred(rows)

            def body(kb, acc, rows=rows, pred=pred):
                return acc + jnp.where(pred(skey_ref[kb, rows, :], kb), 1.0, 0.0)

            acc = lax.fori_loop(0, qi + 1, body, jnp.zeros((half_rows, tq), f32))
            outs.append(jnp.sum(acc, axis=1, keepdims=True))
        return outs

    def wide(ref, rows):
        v = ref[rows, :]
        return jnp.concatenate([v, v], axis=1)

    tu_ref[...] = jnp.zeros_like(tu_ref)

    def bit_step(it, carry):
        bit = jnp.left_shift(jnp.int32(1), 31 - it)

        def make_pred(rows):
            cand = (wide(tu_ref, rows) | bit) ^ INT_MIN
            return lambda key, kb: key >= cand

        cnts = count_where(make_pred)
        for rows, cnt in zip(halves, cnts):
            tu = tu_ref[rows, :]
            tu_ref[rows, :] = jnp.where(cnt >= n_top, tu | bit, tu)
        return carry

    lax.fori_loop(0, 32, bit_step, 0)

    def thr(rows):
        return wide(tu_ref, rows) ^ INT_MIN

    def gt_pred(rows):
        t = thr(rows)
        return lambda key, kb: key > t

    def eq_pred(rows):
        t = thr(rows)
        return lambda key, kb: key == t

    n_gt = count_where(gt_pred)
    n_eq = count_where(eq_pred)
    need = [n_top - g for g in n_gt]
    excess = jnp.maximum(jnp.max(n_eq[0] - need[0]), jnp.max(n_eq[1] - need[1]))
    jb_ref[...] = jnp.full(jb_ref.shape, 2 ** 30, jnp.int32)

    @pl.when(excess > 0.0)
    def _():
        jb_ref[...] = jnp.zeros_like(jb_ref)
        n_bits = max(1, int(math.ceil(math.log2(skey_ref.shape[0] * tq + 1))))

        def jstep(it, carry):
            bit = jnp.left_shift(jnp.int32(1), n_bits - 1 - it)

            def make_pred(rows):
                t = thr(rows)
                bound = wide(jb_ref, rows) | bit
                return lambda key, kb: (key == t) & ((kb * tq + col_iota) < bound)

            cnts = count_where(make_pred)
            for hf, rows in enumerate(halves):
                jb = jb_ref[rows, :]
                ok = (cnts[hf] <= need[hf]) | (n_eq[hf] <= need[hf])
                jb_ref[rows, :] = jnp.where(ok, jb | bit, jb)
            return carry

        lax.fori_loop(0, n_bits, jstep, 0)

    def mask_block(kb, carry):
        for half in range(2):
            rows = slice(half * half_rows, (half + 1) * half_rows)
            key = skey_ref[kb, rows, :]
            t = thr(rows)
            idx = kb * tq + col_iota
            sel = ((key > t) | ((key == t) & (idx < wide(jb_ref, rows)))) & (key != INT_MIN)
            skey_ref[kb, rows, :] = lax.bitcast_convert_type(
                jnp.where(sel, 0.0, MASKED).astype(f32), jnp.int32)
        return carry

    lax.fori_loop(0, qi + 1, mask_block, 0)

    m_ref[...] = jnp.full(m_ref.shape, MASKED, f32)
    l_ref[...] = jnp.zeros_like(l_ref)
    acc_ref[...] = jnp.zeros_like(acc_ref)
    grp = A_HEADS // A_KV_HEADS

    def attend(kb, near):
        r = pl.multiple_of(kb * tq, tq)
        mb = lax.bitcast_convert_type(skey_ref[kb], f32)
        for g in range(A_KV_HEADS):
            kblk = kv_ref[pl.ds(r, tq), g * A_HEAD_DIM:(g + 1) * A_HEAD_DIM]
            vblk = kv_ref[pl.ds(r, tq), A_KV + g * A_HEAD_DIM:A_KV + (g + 1) * A_HEAD_DIM]
            for hh in range(grp):
                h = g * grp + hh
                qh = q_ref[:, h * A_HEAD_DIM:(h + 1) * A_HEAD_DIM]
                s = lax.dot_general(qh, kblk, _NT, preferred_element_type=f32) + mb
                if near is not None:
                    s = s + nb_ref[h, :, near * tq:(near + 1) * tq]
                m_prev = m_ref[h]
                m_new = jnp.maximum(m_prev, jnp.max(s, axis=1, keepdims=True))
                alpha = jnp.exp(m_prev - m_new)
                p = jnp.exp(s - jnp.concatenate([m_new, m_new], axis=1))
                l_ref[h] = alpha * l_ref[h] + jnp.sum(p, axis=1, keepdims=True)
                m_ref[h] = m_new
                acc_ref[h] = alpha * acc_ref[h] + _dot(p.astype(MXU_DTYPE), vblk)

    def far_block(kb, carry):
        attend(kb, None)
        return carry

    lax.fori_loop(0, jnp.maximum(qi - 1, 0), far_block, 0)

    @pl.when(qi >= 1)
    def _():
        attend(qi - 1, 0)

    attend(qi, 1)

    for h in range(A_HEADS):
        o_ref[:, h * A_HEAD_DIM:(h + 1) * A_HEAD_DIM] = (acc_ref[h] / l_ref[h]).astype(o_ref.dtype)


def _t5_bucket(rel):
    half = REL_BUCKETS // 2
    max_exact = half // 2
    ret = jnp.where(rel > 0, half, 0)
    n = jnp.abs(rel)
    nf = jnp.maximum(n, 1).astype(jnp.float32)
    large = max_exact + (jnp.log(nf / max_exact) / math.log(REL_MAX_DIST / max_exact)
                         * (half - max_exact)).astype(jnp.int32)
    large = jnp.minimum(large, half - 1)
    return ret + jnp.where(n < max_exact, n, large)


def _near_bias(rel_bias):
    tq = DSA_TQ
    i = jnp.arange(tq, dtype=jnp.int32)[:, None]
    c = jnp.arange(2 * tq, dtype=jnp.int32)[None, :]
    rel = c - tq - i
    far = rel_bias[_t5_bucket(jnp.full((1,), -REL_MAX_DIST, jnp.int32))]
    nb = rel_bias[_t5_bucket(rel)] - far[None]
    return jnp.transpose(nb, (2, 0, 1)).astype(jnp.float32)


def _dsa(q, kv, iq, ikw, near_bias, batch, seq):
    tq = DSA_TQ
    nq = seq // tq
    n_top = min(TOPK_MAX, seq // 4)
    return pl.pallas_call(
        functools.partial(_dsa_kernel, n_top=n_top),
        grid=(batch, nq),
        in_specs=[pl.BlockSpec((tq, A_Q), lambda b, i: (b * nq + i, 0)),
                  pl.BlockSpec((seq, 2 * A_KV), lambda b, i: (b, 0)),
                  pl.BlockSpec((tq, IDX_Q), lambda b, i: (b * nq + i, 0)),
                  pl.BlockSpec((seq, LANES), lambda b, i: (b, 0)),
                  pl.BlockSpec((tq, LANES), lambda b, i: (b * nq + i, 0)),
                  pl.BlockSpec((A_HEADS, tq, 2 * tq), lambda b, i: (0, 0, 0))],
        out_specs=pl.BlockSpec((tq, A_Q), lambda b, i: (b * nq + i, 0)),
        out_shape=jax.ShapeDtypeStruct((batch * seq, A_Q), MXU_DTYPE),
        scratch_shapes=[pltpu.VMEM((seq, LANES), MXU_DTYPE),
                        pltpu.VMEM((seq, LANES), MXU_DTYPE),
                        pltpu.VMEM((IDX_HEADS, tq, LANES), jnp.float32),
                        pltpu.VMEM((nq, tq, tq), jnp.int32),
                        pltpu.VMEM((tq, LANES), jnp.int32),
                        pltpu.VMEM((tq, LANES), jnp.int32),
                        pltpu.VMEM((A_HEADS, tq, LANES), jnp.float32),
                        pltpu.VMEM((A_HEADS, tq, LANES), jnp.float32),
                        pltpu.VMEM((A_HEADS, tq, A_HEAD_DIM), jnp.float32)],
        compiler_params=_cparams("arbitrary", "arbitrary"), name="dsa")(
            q, kv, iq, ikw, ikw, near_bias)


def _ret_kernel(qk_ref, vg_ref, cos_ref, sin_ref, gn_ref, o_ref, state_ref, dm_ref):
    c_len = RET_CHUNK
    f32 = jnp.float32
    hd = R_HEAD_DIM
    half = hd // 2
    log_g = [math.log(1.0 - 2.0 ** (-5.0 - h)) for h in range(R_HEADS)]

    @pl.when(pl.program_id(1) == 0)
    def _():
        state_ref[...] = jnp.zeros_like(state_ref)
        i = lax.broadcasted_iota(jnp.int32, (c_len, c_len), 0)
        j = lax.broadcasted_iota(jnp.int32, (c_len, c_len), 1)
        diff = (i - j).astype(f32)
        for h in range(R_HEADS):
            dm_ref[h] = jnp.where(diff >= 0, jnp.exp(log_g[h] * jnp.maximum(diff, 0.0)), 0.0)

    row = lax.broadcasted_iota(jnp.int32, (c_len, hd), 0).astype(f32)
    cos = cos_ref[...]
    sin = sin_ref[...]

    def rot(x):
        x1, x2 = x[:, :half], x[:, half:]
        return jnp.concatenate([x1 * cos - x2 * sin, x1 * sin + x2 * cos], axis=1)

    for h in range(R_HEADS):
        cols = slice(h * hd, (h + 1) * hd)
        qf = rot(qk_ref[:, cols].astype(f32))
        kf = rot(qk_ref[:, R_W + h * hd:R_W + (h + 1) * hd].astype(f32)) * (hd ** -0.5)
        v = vg_ref[:, cols]
        att = lax.dot_general(qf.astype(MXU_DTYPE), kf.astype(MXU_DTYPE), _NT,
                              preferred_element_type=f32) * dm_ref[h]
        xi = jnp.exp(log_g[h] * (row + 1.0))
        zeta = jnp.exp(log_g[h] * (c_len - 1.0 - row))
        state = state_ref[h]
        o = _dot(att.astype(MXU_DTYPE), v) + _dot((qf * xi).astype(MXU_DTYPE), state.astype(MXU_DTYPE))
        state_ref[h] = state * math.exp(log_g[h] * c_len) + lax.dot_general(
            (kf * zeta).astype(MXU_DTYPE), v, _TN, preferred_element_type=f32)
        mu = jnp.mean(o, axis=-1, keepdims=True)
        d = o - mu
        var = jnp.mean(d * d, axis=-1, keepdims=True)
        y = d * lax.rsqrt(var + GN_EPS) * gn_ref[:, cols]
        g = vg_ref[:, R_W + h * hd:R_W + (h + 1) * hd].astype(f32)
        o_ref[:, cols] = (g * jax.nn.sigmoid(g) * y).astype(o_ref.dtype)


def _rope_tables(seq):
    half = R_HEAD_DIM // 2
    inv = ROPE_BASE ** (-jnp.arange(half, dtype=jnp.float32) / half)
    ang = jnp.arange(seq, dtype=jnp.float32)[:, None] * inv[None, :]
    return jnp.cos(ang), jnp.sin(ang)


def _retention(rqk, rvg, cos, sin, gn_gain, batch, seq):
    c_len = RET_CHUNK
    nc = seq // c_len
    half = R_HEAD_DIM // 2
    return pl.pallas_call(
        _ret_kernel, grid=(batch, nc),
        in_specs=[pl.BlockSpec((c_len, 2 * R_W), lambda b, c: (b * nc + c, 0)),
                  pl.BlockSpec((c_len, 2 * R_W), lambda b, c: (b * nc + c, 0)),
                  pl.BlockSpec((c_len, half), lambda b, c: (c, 0)),
                  pl.BlockSpec((c_len, half), lambda b, c: (c, 0)),
                  pl.BlockSpec((1, R_W), lambda b, c: (0, 0))],
        out_specs=pl.BlockSpec((c_len, R_W), lambda b, c: (b * nc + c, 0)),
        out_shape=jax.ShapeDtypeStruct((batch * seq, R_W), MXU_DTYPE),
        scratch_shapes=[pltpu.VMEM((R_HEADS, R_HEAD_DIM, R_HEAD_DIM), jnp.float32),
                        pltpu.VMEM((R_HEADS, c_len, c_len), jnp.float32)],
        compiler_params=_cparams("arbitrary", "arbitrary"), name="retention")(
            rqk, rvg, cos, sin, gn_gain.reshape(1, R_W))


def _mixer(h, norm_gain, w_in, q_gain, k_gain, gn_gain, w_out, near_bias, cos, sin, batch, seq):
    xn = _rmsnorm(h, norm_gain)
    w = w_in.astype(MXU_DTYPE)
    c0 = A_Q
    c1 = c0 + 2 * A_KV
    c2 = c1 + IDX_Q
    c3 = c2 + IDX_DIM + IDX_HEADS
    c4 = c3 + 2 * R_W
    q = _project(xn, w[:, :c0], q_gain, norm_heads=4, scale=A_HEAD_DIM ** -0.5)
    kv = _project(xn, w[:, c0:c1], k_gain, norm_heads=A_KV // A_HEAD_DIM)
    iq = _project(xn, w[:, c1:c2])
    w_ikw = jnp.pad(w[:, c2:c3], ((0, 0), (0, LANES - (c3 - c2))))
    ikw = _project(xn, w_ikw, out_dtype=jnp.float32)
    rqk = _project(xn, w[:, c3:c4])
    rvg = _project(xn, w[:, c4:])
    a_out = _dsa(q, kv, iq, ikw, near_bias, batch, seq)
    r_out = _retention(rqk, rvg, cos, sin, gn_gain, batch, seq)
    return _out_project(a_out, r_out, w_out.astype(MXU_DTYPE), h)


def kernel(x, rel_bias, norm_mix, w_in, q_gain, k_gain, ret_gain, w_out, norm_ffn,
           ffn_w_gate, ffn_w_up, ffn_w_down, moe_router, moe_w_gate, moe_w_up, moe_w_down):
    batch, seq, d = x.shape
    depth = norm_mix.shape[0]
    h = x.reshape(batch * seq, d)
    near_bias = _near_bias(rel_bias)
    cos, sin = _rope_tables(seq)
    for l in range(depth):
        h = _mixer(h, norm_mix[l], w_in[l], q_gain[l], k_gain[l], ret_gain[l], w_out[l],
                   near_bias, cos, sin, batch, seq)
        j = l // 2
        if l % 2 == 0:
            hn = _rmsnorm(h, norm_ffn[l])
            hid = _gate_up(hn, ffn_w_gate[j][None].astype(MXU_DTYPE),
                           ffn_w_up[j][None].astype(MXU_DTYPE))
            f = ffn_w_down.shape[1]
            h = _down(hid, ffn_w_down[j].astype(MXU_DTYPE), h, tk=f // 2)
        else:
            hn, gates = _rmsnorm(h, norm_ffn[l], router_w=moe_router[j])
            hid = _gate_up(hn, moe_w_gate[j].astype(MXU_DTYPE), moe_w_up[j].astype(MXU_DTYPE), gates)
            e, f, _ = moe_w_down[j].shape
            h = _down(hid, moe_w_down[j].astype(MXU_DTYPE).reshape(e * f, d), h, tk=f // 2)
    return h.reshape(batch, seq, d)
```

```python
import functools
import math

import jax
import jax.numpy as jnp
import numpy as np
from jax import lax
from jax.experimental import pallas as pl
from jax.experimental.pallas import tpu as pltpu

CHUNK = 64
A_HEADS = 8
A_KV_HEADS = 2
A_HEAD_DIM = 128
IDX_HEADS = 16
IDX_DIM = 64
TOPK_MAX = 256
REL_BUCKETS = 32
REL_MAX_DIST = 128
R_HEADS = 4
R_HEAD_DIM = 256
ROPE_BASE = 10000.0
A_Q = A_HEADS * A_HEAD_DIM
A_KV = A_KV_HEADS * A_HEAD_DIM
IDX_Q = IDX_HEADS * IDX_DIM
R_W = R_HEADS * R_HEAD_DIM
N_EXPERTS = 8
EPS = 1e-6
GN_EPS = 1e-5

LANES = 128
VMEM_LIMIT_BYTES = 56 * 1024 * 1024
MXU_DTYPE = jnp.bfloat16
DSA_TQ = 256
RET_CHUNK = 256
MOE_TM = 512
COMBINE_TM = 256
ROUTE_LANE = 8
INT_MIN = -2147483648
MASKED = -1e30

_NT = (((1,), (1,)), ((), ()))
_TN = (((0,), (0,)), ((), ()))


def _cparams(*sem):
    return pltpu.CompilerParams(dimension_semantics=sem, vmem_limit_bytes=VMEM_LIMIT_BYTES)


def _dot(a, b):
    return jnp.dot(a, b, preferred_element_type=jnp.float32)


def _norm_kernel(x_ref, g_ref, o_ref):
    x = x_ref[...]
    ms = jnp.mean(x * x, axis=-1, keepdims=True)
    o_ref[...] = (x * lax.rsqrt(ms + EPS) * g_ref[...]).astype(o_ref.dtype)


def _norm_router_kernel(x_ref, g_ref, wr_ref, o_ref, gates_ref):
    x = x_ref[...]
    ms = jnp.mean(x * x, axis=-1, keepdims=True)
    xn = x * lax.rsqrt(ms + EPS) * g_ref[...]
    o_ref[...] = xn.astype(o_ref.dtype)
    x_hi = xn.astype(jnp.bfloat16)
    x_lo = (xn - x_hi.astype(jnp.float32)).astype(jnp.bfloat16)
    w = wr_ref[...]
    w_hi = w.astype(jnp.bfloat16)
    w_lo = (w - w_hi.astype(jnp.float32)).astype(jnp.bfloat16)
    logits = _dot(x_hi, w_hi) + (_dot(x_lo, w_hi) + _dot(x_hi, w_lo))
    lane = lax.broadcasted_iota(jnp.int32, logits.shape, 1)
    neg = jnp.float32(-jnp.inf)
    logits = jnp.where(lane < N_EXPERTS, logits, neg)
    m1 = jnp.max(logits, axis=-1, keepdims=True)
    i1 = jnp.min(jnp.where(logits == m1, lane, LANES), axis=-1, keepdims=True)
    rest = jnp.where(lane == i1, neg, logits)
    m2 = jnp.max(rest, axis=-1, keepdims=True)
    i2 = jnp.min(jnp.where(rest == m2, lane, LANES), axis=-1, keepdims=True)
    e2 = jnp.exp(m2 - m1)
    g1 = 1.0 / (1.0 + e2)
    g2 = e2 / (1.0 + e2)
    route = jnp.where(lane == ROUTE_LANE, g1, 0.0) + jnp.where(lane == ROUTE_LANE + 1, g2, 0.0)
    route = route + jnp.where(lane == ROUTE_LANE + 2, i1.astype(jnp.float32), 0.0)
    route = route + jnp.where(lane == ROUTE_LANE + 3, i2.astype(jnp.float32), 0.0)
    gates_ref[...] = route


def _rmsnorm(x, gain, router_w=None, tm=512):
    n, d = x.shape
    grid = (n // tm,)
    x_spec = pl.BlockSpec((tm, d), lambda i: (i, 0))
    g_spec = pl.BlockSpec((1, d), lambda i: (0, 0))
    o_spec = pl.BlockSpec((tm, d), lambda i: (i, 0))
    gain2 = gain.reshape(1, d)
    if router_w is None:
        return pl.pallas_call(
            _norm_kernel, grid=grid, in_specs=[x_spec, g_spec], out_specs=o_spec,
            out_shape=jax.ShapeDtypeStruct((n, d), MXU_DTYPE),
            compiler_params=_cparams("parallel"), name="rmsnorm")(x, gain2)
    wr = jnp.pad(router_w, ((0, 0), (0, LANES - router_w.shape[1])))
    return pl.pallas_call(
        _norm_router_kernel, grid=grid,
        in_specs=[x_spec, g_spec, pl.BlockSpec((d, LANES), lambda i: (0, 0))],
        out_specs=[o_spec, pl.BlockSpec((tm, LANES), lambda i: (i, 0))],
        out_shape=[jax.ShapeDtypeStruct((n, d), jnp.float32),
                   jax.ShapeDtypeStruct((n, LANES), jnp.float32)],
        compiler_params=_cparams("parallel"), name="rmsnorm_router")(x, gain2, wr)


def _proj_kernel(x_ref, w_ref, gain_ref, o_ref, *, norm_heads, scale):
    acc = _dot(x_ref[...], w_ref[...])
    tn = acc.shape[1]
    for h in range(tn // A_HEAD_DIM):
        y = acc[:, h * A_HEAD_DIM:(h + 1) * A_HEAD_DIM]
        if h < norm_heads:
            ms = jnp.mean(y * y, axis=-1, keepdims=True)
            y = y * lax.rsqrt(ms + EPS) * (gain_ref[...] * scale)
        o_ref[:, h * A_HEAD_DIM:(h + 1) * A_HEAD_DIM] = y.astype(o_ref.dtype)


def _project(x, w, gain=None, *, norm_heads=0, scale=1.0, out_dtype=None, tm=1024, tn=512):
    out_dtype = out_dtype or MXU_DTYPE
    n, d = x.shape
    f = w.shape[1]
    tn = min(tn, f)
    if gain is None:
        gain = jnp.ones((A_HEAD_DIM,), jnp.float32)
    return pl.pallas_call(
        functools.partial(_proj_kernel, norm_heads=norm_heads, scale=scale),
        grid=(n // tm, f // tn),
        in_specs=[pl.BlockSpec((tm, d), lambda i, j: (i, 0)),
                  pl.BlockSpec((d, tn), lambda i, j: (0, j)),
                  pl.BlockSpec((1, A_HEAD_DIM), lambda i, j: (0, 0))],
        out_specs=pl.BlockSpec((tm, tn), lambda i, j: (i, j)),
        out_shape=jax.ShapeDtypeStruct((n, f), out_dtype),
        compiler_params=_cparams("parallel", "parallel"), name="in_proj")(
            x, w, gain.reshape(1, A_HEAD_DIM))


def _outproj_kernel(a_ref, r_ref, wa_ref, wr_ref, res_ref, o_ref):
    o_ref[...] = res_ref[...] + (_dot(a_ref[...], wa_ref[...]) + _dot(r_ref[...], wr_ref[...]))


def _out_project(a, r, w, res, tm=1024, tn=512):
    n, ka = a.shape
    kr = r.shape[1]
    assert ka == kr
    d = w.shape[1]
    tn = min(tn, d)
    return pl.pallas_call(
        _outproj_kernel, grid=(n // tm, d // tn),
        in_specs=[pl.BlockSpec((tm, ka), lambda i, j: (i, 0)),
                  pl.BlockSpec((tm, kr), lambda i, j: (i, 0)),
                  pl.BlockSpec((ka, tn), lambda i, j: (0, j)),
                  pl.BlockSpec((kr, tn), lambda i, j: (1, j)),
                  pl.BlockSpec((tm, tn), lambda i, j: (i, j))],
        out_specs=pl.BlockSpec((tm, tn), lambda i, j: (i, j)),
        out_shape=jax.ShapeDtypeStruct((n, d), jnp.float32),
        compiler_params=_cparams("parallel", "parallel"), name="out_proj")(a, r, w, w, res)


def _swiglu_hidden(x, wg, wu):
    g = _dot(x, wg)
    u = _dot(x, wu)
    return g * jax.nn.sigmoid(g) * u


def _gu_kernel(x_ref, wg_ref, wu_ref, o_ref):
    o_ref[...] = _swiglu_hidden(x_ref[...], wg_ref[...], wu_ref[...]).astype(o_ref.dtype)


def _gate_up(x, wg, wu, tm=1024, tn=512):
    n, d = x.shape
    f = wg.shape[1]
    return pl.pallas_call(
        _gu_kernel, grid=(n // tm, f // tn),
        in_specs=[pl.BlockSpec((tm, d), lambda i, j: (i, 0)),
                  pl.BlockSpec((d, tn), lambda i, j: (0, j)),
                  pl.BlockSpec((d, tn), lambda i, j: (0, j))],
        out_specs=pl.BlockSpec((tm, tn), lambda i, j: (i, j)),
        out_shape=jax.ShapeDtypeStruct((n, f), MXU_DTYPE),
        compiler_params=_cparams("parallel", "parallel"), name="gate_up")(x, wg, wu)


def _down_kernel(x_ref, w_ref, res_ref, o_ref):
    @pl.when(pl.program_id(2) == 0)
    def _():
        o_ref[...] = res_ref[...]
    o_ref[...] += _dot(x_ref[...], w_ref[...])


def _down(x, w, res, tk, tm=512, tn=1024):
    n, kt = x.shape
    d = w.shape[1]
    return pl.pallas_call(
        _down_kernel, grid=(n // tm, d // tn, kt // tk),
        in_specs=[pl.BlockSpec((tm, tk), lambda i, j, k: (i, k)),
                  pl.BlockSpec((tk, tn), lambda i, j, k: (k, j)),
                  pl.BlockSpec((tm, tn), lambda i, j, k: (i, j))],
        out_specs=pl.BlockSpec((tm, tn), lambda i, j, k: (i, j)),
        out_shape=jax.ShapeDtypeStruct((n, d), jnp.float32),
        compiler_params=_cparams("parallel", "parallel", "arbitrary"), name="down_proj")(x, w, res)


def _route_tables(route, tm):
    n = route.shape[0]
    e = route[:, ROUTE_LANE + 2:ROUTE_LANE + 4].astype(jnp.int32).reshape(-1)
    onehot = (e[:, None] == jnp.arange(N_EXPERTS, dtype=jnp.int32)[None, :]).astype(jnp.int32)
    csum = jnp.cumsum(onehot, axis=0)
    rank = jnp.sum(csum * onehot, axis=1) - 1
    tiles_per = (csum[-1] + tm - 1) // tm
    tile_end = jnp.cumsum(tiles_per)
    row_start = (tile_end - tiles_per) * tm
    pos = jnp.sum(onehot * row_start[None, :], axis=1) + rank
    n_tiles = (2 * n) // tm + N_EXPERTS
    t = jnp.arange(n_tiles, dtype=jnp.int32)
    tile_expert = jnp.sum((t[:, None] >= tile_end[None, :]).astype(jnp.int32), axis=1)
    meta = jnp.concatenate([jnp.minimum(tile_expert, N_EXPERTS - 1), tile_end[-1:]]).astype(jnp.int32)
    src = jnp.zeros((n_tiles * tm,), jnp.int32).at[pos].set(
        jnp.arange(2 * n, dtype=jnp.int32) // 2, unique_indices=True)
    return src.reshape(n_tiles, 1, tm), pos.reshape(n, 2), meta


def _row_copy(src_hbm, row, dst, slot, r, sem):
    return pltpu.make_async_copy(src_hbm.at[pl.ds(row, 1)], dst.at[slot, pl.ds(r, 1)], sem)


def _gather_kernel(idx_ref, nxt_ref, x_hbm, o_ref, buf, sem):
    i = pl.program_id(0)
    tm = buf.shape[1]
    slot = lax.rem(i, 2)

    def issue(ref, s):
        def body(r, c):
            _row_copy(x_hbm, ref[0, 0, r], buf, s, r, sem.at[s]).start()
            return c
        lax.fori_loop(0, tm, body, 0)

    @pl.when(i == 0)
    def _():
        issue(idx_ref, 0)

    @pl.when(i + 1 < pl.num_programs(0))
    def _():
        issue(nxt_ref, 1 - slot)

    pltpu.make_async_copy(x_hbm.at[pl.ds(0, tm)], buf.at[slot], sem.at[slot]).wait()
    o_ref[...] = buf[slot].astype(o_ref.dtype)


def _moe_gather(x, src):
    n_tiles, _, tm = src.shape
    d = x.shape[1]
    idx_spec = pl.BlockSpec((1, 1, tm), lambda i: (i, 0, 0), memory_space=pltpu.SMEM)
    nxt_spec = pl.BlockSpec((1, 1, tm), lambda i: (jnp.minimum(i + 1, n_tiles - 1), 0, 0),
                            memory_space=pltpu.SMEM)
    return pl.pallas_call(
        _gather_kernel, grid=(n_tiles,),
        in_specs=[idx_spec, nxt_spec, pl.BlockSpec(memory_space=pl.ANY)],
        out_specs=pl.BlockSpec((tm, d), lambda i: (i, 0)),
        out_shape=jax.ShapeDtypeStruct((n_tiles * tm, d), MXU_DTYPE),
        scratch_shapes=[pltpu.VMEM((2, tm, d), x.dtype), pltpu.SemaphoreType.DMA((2,))],
        compiler_params=_cparams("arbitrary"), name="moe_gather")(src, src, x)


def _moe_gu_kernel(meta_ref, x_ref, wg_ref, wu_ref, o_ref):
    used = pl.program_id(1) < meta_ref[pl.num_programs(1)]

    @pl.when(used)
    def _():
        o_ref[...] = _swiglu_hidden(x_ref[...], wg_ref[0], wu_ref[0]).astype(o_ref.dtype)

    @pl.when(jnp.logical_not(used))
    def _():
        o_ref[...] = jnp.zeros_like(o_ref)


def _moe_gate_up(xs, wg, wu, meta, tm, tn=1024):
    p, d = xs.shape
    f = wg.shape[2]
    tn = min(tn, f)
    grid_spec = pltpu.PrefetchScalarGridSpec(
        num_scalar_prefetch=1, grid=(f // tn, p // tm),
        in_specs=[pl.BlockSpec((tm, d), lambda j, i, m: (i, 0)),
                  pl.BlockSpec((1, d, tn), lambda j, i, m: (m[i], 0, j)),
                  pl.BlockSpec((1, d, tn), lambda j, i, m: (m[i], 0, j))],
        out_specs=pl.BlockSpec((tm, tn), lambda j, i, m: (i, j)))
    return pl.pallas_call(
        _moe_gu_kernel, grid_spec=grid_spec,
        out_shape=jax.ShapeDtypeStruct((p, f), MXU_DTYPE),
        compiler_params=_cparams("parallel", "parallel"), name="moe_gate_up")(meta, xs, wg, wu)


def _moe_down_kernel(meta_ref, x_ref, w_ref, o_ref):
    used = pl.program_id(1) < meta_ref[pl.num_programs(1)]

    @pl.when(used)
    def _():
        o_ref[...] = _dot(x_ref[...], w_ref[0])

    @pl.when(jnp.logical_not(used))
    def _():
        o_ref[...] = jnp.zeros_like(o_ref)


def _moe_down(hid, wd, meta, tm, tn=512):
    p, f = hid.shape
    d = wd.shape[2]
    tn = min(tn, d)
    grid_spec = pltpu.PrefetchScalarGridSpec(
        num_scalar_prefetch=1, grid=(d // tn, p // tm),
        in_specs=[pl.BlockSpec((tm, f), lambda j, i, m: (i, 0)),
                  pl.BlockSpec((1, f, tn), lambda j, i, m: (m[i], 0, j))],
        out_specs=pl.BlockSpec((tm, tn), lambda j, i, m: (i, j)))
    return pl.pallas_call(
        _moe_down_kernel, grid_spec=grid_spec,
        out_shape=jax.ShapeDtypeStruct((p, d), jnp.float32),
        compiler_params=_cparams("parallel", "parallel"), name="moe_down")(meta, hid, wd)


def _combine_kernel(pa_ref, pb_ref, pa_nxt, pb_nxt, y_hbm, h_ref, route_ref, o_ref, bufa, bufb, sem):
    i = pl.program_id(0)
    tm = bufa.shape[1]
    slot = lax.rem(i, 2)

    def issue(pa, pb, s):
        def body(r, c):
            _row_copy(y_hbm, pa[0, 0, r], bufa, s, r, sem.at[0, s]).start()
            _row_copy(y_hbm, pb[0, 0, r], bufb, s, r, sem.at[1, s]).start()
            return c
        lax.fori_loop(0, tm, body, 0)

    @pl.when(i == 0)
    def _():
        issue(pa_ref, pb_ref, 0)

    @pl.when(i + 1 < pl.num_programs(0))
    def _():
        issue(pa_nxt, pb_nxt, 1 - slot)

    pltpu.make_async_copy(y_hbm.at[pl.ds(0, tm)], bufa.at[slot], sem.at[0, slot]).wait()
    pltpu.make_async_copy(y_hbm.at[pl.ds(0, tm)], bufb.at[slot], sem.at[1, slot]).wait()
    route = route_ref[...]
    ga = route[:, ROUTE_LANE:ROUTE_LANE + 1]
    gb = route[:, ROUTE_LANE + 1:ROUTE_LANE + 2]
    o_ref[...] = h_ref[...] + (ga * bufa[slot] + gb * bufb[slot])


def _moe_combine(y, pos, h, route, tm):
    n, d = h.shape
    nt = n // tm
    pa = pos[:, 0].reshape(nt, 1, tm)
    pb = pos[:, 1].reshape(nt, 1, tm)
    cur = pl.BlockSpec((1, 1, tm), lambda i: (i, 0, 0), memory_space=pltpu.SMEM)
    nxt = pl.BlockSpec((1, 1, tm), lambda i: (jnp.minimum(i + 1, nt - 1), 0, 0), memory_space=pltpu.SMEM)
    return pl.pallas_call(
        _combine_kernel, grid=(nt,),
        in_specs=[cur, cur, nxt, nxt, pl.BlockSpec(memory_space=pl.ANY),
                  pl.BlockSpec((tm, d), lambda i: (i, 0)),
                  pl.BlockSpec((tm, LANES), lambda i: (i, 0))],
        out_specs=pl.BlockSpec((tm, d), lambda i: (i, 0)),
        out_shape=jax.ShapeDtypeStruct((n, d), jnp.float32),
        scratch_shapes=[pltpu.VMEM((2, tm, d), jnp.float32), pltpu.VMEM((2, tm, d), jnp.float32),
                        pltpu.SemaphoreType.DMA((2, 2))],
        compiler_params=_cparams("arbitrary"), name="moe_combine")(pa, pb, pa, pb, y, h, route)


def _moe(h, norm_gain, router_w, wg, wu, wd):
    hn, route = _rmsnorm(h, norm_gain, router_w=router_w)
    src, pos, meta = _route_tables(route, MOE_TM)
    xs = _moe_gather(hn, src)
    hid = _moe_gate_up(xs, wg.astype(MXU_DTYPE), wu.astype(MXU_DTYPE), meta, MOE_TM)
    y = _moe_down(hid, wd.astype(MXU_DTYPE), meta, MOE_TM)
    return _moe_combine(y, pos, h, route, COMBINE_TM)


def _sortable(x):
    x = jnp.where(x == 0.0, 0.0, x)
    i = lax.bitcast_convert_type(x, jnp.int32)
    return i ^ ((i >> 31) & jnp.int32(0x7FFFFFFF))


def _dsa_kernel(q_ref, kv_ref, iq_ref, ikk_ref, ikq_ref, nb_ref, o_ref,
                iklo_ref, ikhi_ref, wb_ref, skey_ref, tu_ref, jb_ref, m_ref, l_ref, acc_ref,
                *, n_top):
    tq = DSA_TQ
    half_rows = tq // 2
    qi = pl.program_id(1)
    f32 = jnp.float32

    @pl.when(qi == 0)
    def _():
        ik = ikk_ref[...]
        lane = lax.broadcasted_iota(jnp.int32, ik.shape, 1)
        iklo_ref[...] = jnp.where(lane < IDX_DIM, ik, 0.0).astype(MXU_DTYPE)
        shifted = pltpu.roll(ik, IDX_DIM, axis=1)
        ikhi_ref[...] = jnp.where(lane >= IDX_DIM, shifted, 0.0).astype(MXU_DTYPE)

    ikq = ikq_ref[...]
    for h in range(IDX_HEADS):
        wb_ref[h] = jnp.broadcast_to(ikq[:, IDX_DIM + h:IDX_DIM + h + 1], (tq, LANES))

    row_pos = qi * tq + lax.broadcasted_iota(jnp.int32, (half_rows, tq), 0)
    col_iota = lax.broadcasted_iota(jnp.int32, (half_rows, tq), 1)

    def score_block(kb, carry):
        r = pl.multiple_of(kb * tq, tq)
        klo = iklo_ref[pl.ds(r, tq), :]
        khi = ikhi_ref[pl.ds(r, tq), :]
        for half in range(2):
            rows = slice(half * half_rows, (half + 1) * half_rows)
            acc = jnp.zeros((half_rows, tq), f32)
            for p in range(IDX_HEADS // 2):
                lhs = iq_ref[rows, p * LANES:(p + 1) * LANES]
                for rhs, h in ((klo, 2 * p), (khi, 2 * p + 1)):
                    hs = lax.dot_general(lhs, rhs, _NT, preferred_element_type=f32)
                    w = wb_ref[h, rows, :]
                    acc = acc + jnp.maximum(hs, 0.0) * jnp.concatenate([w, w], axis=1)
            key = _sortable(acc)
            t_chunk = (row_pos + half * half_rows) // CHUNK
            s_chunk = (kb * tq + col_iota) // CHUNK
            skey_ref[kb, rows, :] = jnp.where(s_chunk <= t_chunk, key, INT_MIN)
        return carry

    lax.fori_loop(0, qi + 1, score_block, 0)

    halves = [slice(hf * half_rows, (hf + 1) * half_rows) for hf in range(2)]

    def count_where(make_pred):
        outs = []
        for rows in halves:
            pred = make_pred(rows)

            def body(kb, acc, rows=rows, pred=pred):
                return acc + jnp.where(pred(skey_ref[kb, rows, :], kb), 1.0, 0.0)

            acc = lax.fori_loop(0, qi + 1, body, jnp.zeros((half_rows, tq), f32))
            outs.append(jnp.sum(acc, axis=1, keepdims=True))
        return outs

    def wide(ref, rows):
        v = ref[rows, :]
        return jnp.concatenate([v, v], axis=1)

    tu_ref[...] = jnp.zeros_like(tu_ref)

    def bit_step(it, carry):
        bit = jnp.left_shift(jnp.int32(1), 31 - it)

        def make_pred(rows):
            cand = (wide(tu_ref, rows) | bit) ^ INT_MIN
            return lambda key, kb: key >= cand

        cnts = count_where(make_pred)
        for rows, cnt in zip(halves, cnts):
            tu = tu_ref[rows, :]
            tu_ref[rows, :] = jnp.where(cnt >= n_top, tu | bit, tu)
        return carry

    lax.fori_loop(0, 32, bit_step, 0)

    def thr(rows):
        return wide(tu_ref, rows) ^ INT_MIN

    def gt_pred(rows):
        t = thr(rows)
        return lambda key, kb: key > t

    def eq_pred(rows):
        t = thr(rows)
        return lambda key, kb: key == t

    n_gt = count_where(gt_pred)
    n_eq = count_where(eq_pred)
    need = [n_top - g for g in n_gt]
    excess = jnp.maximum(jnp.max(n_eq[0] - need[0]), jnp.max(n_eq[1] - need[1]))
    jb_ref[...] = jnp.full(jb_ref.shape, 2 ** 30, jnp.int32)

    @pl.when(excess > 0.0)
    def _():
        jb_ref[...] = jnp.zeros_like(jb_ref)
        n_bits = max(1, int(math.ceil(math.log2(skey_ref.shape[0] * tq + 1))))

        def jstep(it, carry):
            bit = jnp.left_shift(jnp.int32(1), n_bits - 1 - it)

            def make_pred(rows):
                t = thr(rows)
                bound = wide(jb_ref, rows) | bit
                return lambda key, kb: (key == t) & ((kb * tq + col_iota) < bound)

            cnts = count_where(make_pred)
            for hf, rows in enumerate(halves):
                jb = jb_ref[rows, :]
                ok = (cnts[hf] <= need[hf]) | (n_eq[hf] <= need[hf])
                jb_ref[rows, :] = jnp.where(ok, jb | bit, jb)
            return carry

        lax.fori_loop(0, n_bits, jstep, 0)

    def mask_block(kb, carry):
        for half in range(2):
            rows = slice(half * half_rows, (half + 1) * half_rows)
            key = skey_ref[kb, rows, :]
            t = thr(rows)
            idx = kb * tq + col_iota
            sel = ((key > t) | ((key == t) & (idx < wide(jb_ref, rows)))) & (key != INT_MIN)
            skey_ref[kb, rows, :] = lax.bitcast_convert_type(
                jnp.where(sel, 0.0, MASKED).astype(f32), jnp.int32)
        return carry

    lax.fori_loop(0, qi + 1, mask_block, 0)

    m_ref[...] = jnp.full(m_ref.shape, MASKED, f32)
    l_ref[...] = jnp.zeros_like(l_ref)
    acc_ref[...] = jnp.zeros_like(acc_ref)
    grp = A_HEADS // A_KV_HEADS

    def attend(kb, near):
        r = pl.multiple_of(kb * tq, tq)
        mb = lax.bitcast_convert_type(skey_ref[kb], f32)
        for g in range(A_KV_HEADS):
            kblk = kv_ref[pl.ds(r, tq), g * A_HEAD_DIM:(g + 1) * A_HEAD_DIM]
            vblk = kv_ref[pl.ds(r, tq), A_KV + g * A_HEAD_DIM:A_KV + (g + 1) * A_HEAD_DIM]
            for hh in range(grp):
                h = g * grp + hh
                qh = q_ref[:, h * A_HEAD_DIM:(h + 1) * A_HEAD_DIM]
                s = lax.dot_general(qh, kblk, _NT, preferred_element_type=f32) + mb
                if near is not None:
                    s = s + nb_ref[h, :, near * tq:(near + 1) * tq]
                m_prev = m_ref[h]
                m_new = jnp.maximum(m_prev, jnp.max(s, axis=1, keepdims=True))
                alpha = jnp.exp(m_prev - m_new)
                p = jnp.exp(s - jnp.concatenate([m_new, m_new], axis=1))
                l_ref[h] = alpha * l_ref[h] + jnp.sum(p, axis=1, keepdims=True)
                m_ref[h] = m_new
                acc_ref[h] = alpha * acc_ref[h] + _dot(p.astype(MXU_DTYPE), vblk)

    def far_block(kb, carry):
        attend(kb, None)
        return carry

    lax.fori_loop(0, jnp.maximum(qi - 1, 0), far_block, 0)

    @pl.when(qi >= 1)
    def _():
        attend(qi - 1, 0)

    attend(qi, 1)

    for h in range(A_HEADS):
        o_ref[:, h * A_HEAD_DIM:(h + 1) * A_HEAD_DIM] = (acc_ref[h] / l_ref[h]).astype(o_ref.dtype)


def _t5_bucket(rel):
    half = REL_BUCKETS // 2
    max_exact = half // 2
    ret = jnp.where(rel > 0, half, 0)
    n = jnp.abs(rel)
    nf = jnp.maximum(n, 1).astype(jnp.float32)
    large = max_exact + (jnp.log(nf / max_exact) / math.log(REL_MAX_DIST / max_exact)
                         * (half - max_exact)).astype(jnp.int32)
    large = jnp.minimum(large, half - 1)
    return ret + jnp.where(n < max_exact, n, large)


def _near_bias(rel_bias):
    tq = DSA_TQ
    i = jnp.arange(tq, dtype=jnp.int32)[:, None]
    c = jnp.arange(2 * tq, dtype=jnp.int32)[None, :]
    rel = c - tq - i
    far = rel_bias[_t5_bucket(jnp.full((1,), -REL_MAX_DIST, jnp.int32))]
    onehot = (_t5_bucket(rel)[None] == jnp.arange(REL_BUCKETS, dtype=jnp.int32)[:, None, None])
    table = (rel_bias - far).T
    return jnp.einsum('hb,bic->hic', table, onehot.astype(jnp.float32),
                      precision=lax.Precision.HIGHEST)


def _dsa(q, kv, iq, ikw, near_bias, batch, seq):
    tq = DSA_TQ
    nq = seq // tq
    n_top = min(TOPK_MAX, seq // 4)
    return pl.pallas_call(
        functools.partial(_dsa_kernel, n_top=n_top),
        grid=(batch, nq),
        in_specs=[pl.BlockSpec((tq, A_Q), lambda b, i: (b * nq + i, 0)),
                  pl.BlockSpec((seq, 2 * A_KV), lambda b, i: (b, 0)),
                  pl.BlockSpec((tq, IDX_Q), lambda b, i: (b * nq + i, 0)),
                  pl.BlockSpec((seq, LANES), lambda b, i: (b, 0)),
                  pl.BlockSpec((tq, LANES), lambda b, i: (b * nq + i, 0)),
                  pl.BlockSpec((A_HEADS, tq, 2 * tq), lambda b, i: (0, 0, 0))],
        out_specs=pl.BlockSpec((tq, A_Q), lambda b, i: (b * nq + i, 0)),
        out_shape=jax.ShapeDtypeStruct((batch * seq, A_Q), MXU_DTYPE),
        scratch_shapes=[pltpu.VMEM((seq, LANES), MXU_DTYPE),
                        pltpu.VMEM((seq, LANES), MXU_DTYPE),
                        pltpu.VMEM((IDX_HEADS, tq, LANES), jnp.float32),
                        pltpu.VMEM((nq, tq, tq), jnp.int32),
                        pltpu.VMEM((tq, LANES), jnp.int32),
                        pltpu.VMEM((tq, LANES), jnp.int32),
                        pltpu.VMEM((A_HEADS, tq, LANES), jnp.float32),
                        pltpu.VMEM((A_HEADS, tq, LANES), jnp.float32),
                        pltpu.VMEM((A_HEADS, tq, A_HEAD_DIM), jnp.float32)],
        compiler_params=_cparams("arbitrary", "arbitrary"), name="dsa")(
            q, kv, iq, ikw, ikw, near_bias)


def _ret_kernel(qk_ref, vg_ref, cos_ref, sin_ref, gn_ref, o_ref, state_ref, dm_ref):
    c_len = RET_CHUNK
    f32 = jnp.float32
    hd = R_HEAD_DIM
    half = hd // 2
    log_g = [math.log(1.0 - 2.0 ** (-5.0 - h)) for h in range(R_HEADS)]

    @pl.when(pl.program_id(1) == 0)
    def _():
        state_ref[...] = jnp.zeros_like(state_ref)
        i = lax.broadcasted_iota(jnp.int32, (c_len, c_len), 0)
        j = lax.broadcasted_iota(jnp.int32, (c_len, c_len), 1)
        diff = (i - j).astype(f32)
        for h in range(R_HEADS):
            dm_ref[h] = jnp.where(diff >= 0, jnp.exp(log_g[h] * jnp.maximum(diff, 0.0)), 0.0)

    row = lax.broadcasted_iota(jnp.int32, (c_len, hd), 0).astype(f32)
    cos = cos_ref[...]
    sin = sin_ref[...]

    def rot(x):
        x1, x2 = x[:, :half], x[:, half:]
        return jnp.concatenate([x1 * cos - x2 * sin, x1 * sin + x2 * cos], axis=1)

    for h in range(R_HEADS):
        cols = slice(h * hd, (h + 1) * hd)
        qf = rot(qk_ref[:, cols].astype(f32))
        kf = rot(qk_ref[:, R_W + h * hd:R_W + (h + 1) * hd].astype(f32)) * (hd ** -0.5)
        v = vg_ref[:, cols]
        att = lax.dot_general(qf.astype(MXU_DTYPE), kf.astype(MXU_DTYPE), _NT,
                              preferred_element_type=f32) * dm_ref[h]
        xi = jnp.exp(log_g[h] * (row + 1.0))
        zeta = jnp.exp(log_g[h] * (c_len - 1.0 - row))
        state = state_ref[h]
        o = _dot(att.astype(MXU_DTYPE), v) + _dot((qf * xi).astype(MXU_DTYPE), state.astype(MXU_DTYPE))
        state_ref[h] = state * math.exp(log_g[h] * c_len) + lax.dot_general(
            (kf * zeta).astype(MXU_DTYPE), v, _TN, preferred_element_type=f32)
        mu = jnp.mean(o, axis=-1, keepdims=True)
        d = o - mu
        var = jnp.mean(d * d, axis=-1, keepdims=True)
        y = d * lax.rsqrt(var + GN_EPS) * gn_ref[:, cols]
        g = vg_ref[:, R_W + h * hd:R_W + (h + 1) * hd].astype(f32)
        o_ref[:, cols] = (g * jax.nn.sigmoid(g) * y).astype(o_ref.dtype)


def _rope_tables(seq):
    half = R_HEAD_DIM // 2
    inv = ROPE_BASE ** (-jnp.arange(half, dtype=jnp.float32) / half)
    ang = jnp.arange(seq, dtype=jnp.float32)[:, None] * inv[None, :]
    return jnp.cos(ang), jnp.sin(ang)


def _retention(rqk, rvg, cos, sin, gn_gain, batch, seq):
    c_len = RET_CHUNK
    nc = seq // c_len
    half = R_HEAD_DIM // 2
    return pl.pallas_call(
        _ret_kernel, grid=(batch, nc),
        in_specs=[pl.BlockSpec((c_len, 2 * R_W), lambda b, c: (b * nc + c, 0)),
                  pl.BlockSpec((c_len, 2 * R_W), lambda b, c: (b * nc + c, 0)),
                  pl.BlockSpec((c_len, half), lambda b, c: (c, 0)),
                  pl.BlockSpec((c_len, half), lambda b, c: (c, 0)),
                  pl.BlockSpec((1, R_W), lambda b, c: (0, 0))],
        out_specs=pl.BlockSpec((c_len, R_W), lambda b, c: (b * nc + c, 0)),
        out_shape=jax.ShapeDtypeStruct((batch * seq, R_W), MXU_DTYPE),
        scratch_shapes=[pltpu.VMEM((R_HEADS, R_HEAD_DIM, R_HEAD_DIM), jnp.float32),
                        pltpu.VMEM((R_HEADS, c_len, c_len), jnp.float32)],
        compiler_params=_cparams("arbitrary", "arbitrary"), name="retention")(
            rqk, rvg, cos, sin, gn_gain.reshape(1, R_W))


def _mixer(h, norm_gain, w_in, q_gain, k_gain, gn_gain, w_out, near_bias, cos, sin, batch, seq):
    xn = _rmsnorm(h, norm_gain)
    w = w_in.astype(MXU_DTYPE)
    c0 = A_Q
    c1 = c0 + 2 * A_KV
    c2 = c1 + IDX_Q
    c3 = c2 + IDX_DIM + IDX_HEADS
    c4 = c3 + 2 * R_W
    q = _project(xn, w[:, :c0], q_gain, norm_heads=4, scale=A_HEAD_DIM ** -0.5)
    kv = _project(xn, w[:, c0:c1], k_gain, norm_heads=A_KV // A_HEAD_DIM)
    iq = _project(xn, w[:, c1:c2])
    w_ikw = jnp.pad(w[:, c2:c3], ((0, 0), (0, LANES - (c3 - c2))))
    ikw = _project(xn, w_ikw, out_dtype=jnp.float32)
    rqk = _project(xn, w[:, c3:c4])
    rvg = _project(xn, w[:, c4:])
    a_out = _dsa(q, kv, iq, ikw, near_bias, batch, seq)
    r_out = _retention(rqk, rvg, cos, sin, gn_gain, batch, seq)
    return _out_project(a_out, r_out, w_out.astype(MXU_DTYPE), h)


def kernel(x, rel_bias, norm_mix, w_in, q_gain, k_gain, ret_gain, w_out, norm_ffn,
           ffn_w_gate, ffn_w_up, ffn_w_down, moe_router, moe_w_gate, moe_w_up, moe_w_down):
    batch, seq, d = x.shape
    depth = norm_mix.shape[0]
    h = x.reshape(batch * seq, d)
    near_bias = _near_bias(rel_bias)
    cos, sin = _rope_tables(seq)
    for l in range(depth):
        h = _mixer(h, norm_mix[l], w_in[l], q_gain[l], k_gain[l], ret_gain[l], w_out[l],
                   near_bias, cos, sin, batch, seq)
        j = l // 2
        if l % 2 == 0:
            hn = _rmsnorm(h, norm_ffn[l])
            hid = _gate_up(hn, ffn_w_gate[j].astype(MXU_DTYPE), ffn_w_up[j].astype(MXU_DTYPE))
            f = ffn_w_down.shape[1]
            h = _down(hid, ffn_w_down[j].astype(MXU_DTYPE), h, tk=f // 2)
        else:
            h = _moe(h, norm_ffn[l], moe_router[j], moe_w_gate[j], moe_w_up[j], moe_w_down[j])
    return h.reshape(batch, seq, d)
```

```python
import functools
import math

import jax
import jax.numpy as jnp
import numpy as np
from jax import lax
from jax.experimental import pallas as pl
from jax.experimental.pallas import tpu as pltpu

CHUNK = 64
A_HEADS = 8
A_KV_HEADS = 2
A_HEAD_DIM = 128
IDX_HEADS = 16
IDX_DIM = 64
TOPK_MAX = 256
REL_BUCKETS = 32
REL_MAX_DIST = 128
R_HEADS = 4
R_HEAD_DIM = 256
ROPE_BASE = 10000.0
A_Q = A_HEADS * A_HEAD_DIM
A_KV = A_KV_HEADS * A_HEAD_DIM
IDX_Q = IDX_HEADS * IDX_DIM
R_W = R_HEADS * R_HEAD_DIM
N_EXPERTS = 8
EPS = 1e-6
GN_EPS = 1e-5

LANES = 128
SUBLANES = 8
LOG2E = math.log2(math.e)
VMEM_LIMIT_BYTES = 56 * 1024 * 1024
MXU_DTYPE = jnp.bfloat16
DSA_TQ = 256
RET_CHUNK = 256
MOE_TM = 512
COMBINE_TM = 256
ROUTE_LANE = 8
INT_MIN = -2147483648
MASKED = -1e30

_NT = (((1,), (1,)), ((), ()))
_TN = (((0,), (0,)), ((), ()))


def _cparams(*sem):
    return pltpu.CompilerParams(dimension_semantics=sem, vmem_limit_bytes=VMEM_LIMIT_BYTES)


def _dot(a, b):
    return jnp.dot(a, b, preferred_element_type=jnp.float32)


def _norm_kernel(x_ref, g_ref, o_ref):
    x = x_ref[...]
    ms = jnp.mean(x * x, axis=-1, keepdims=True)
    o_ref[...] = (x * lax.rsqrt(ms + EPS) * g_ref[...]).astype(o_ref.dtype)


def _norm_router_kernel(x_ref, g_ref, wr_ref, o_ref, gates_ref):
    x = x_ref[...]
    ms = jnp.mean(x * x, axis=-1, keepdims=True)
    xn = x * lax.rsqrt(ms + EPS) * g_ref[...]
    o_ref[...] = xn.astype(o_ref.dtype)
    x_hi = xn.astype(jnp.bfloat16)
    x_lo = (xn - x_hi.astype(jnp.float32)).astype(jnp.bfloat16)
    w = wr_ref[...]
    w_hi = w.astype(jnp.bfloat16)
    w_lo = (w - w_hi.astype(jnp.float32)).astype(jnp.bfloat16)
    logits = _dot(x_hi, w_hi) + (_dot(x_lo, w_hi) + _dot(x_hi, w_lo))
    lane = lax.broadcasted_iota(jnp.int32, logits.shape, 1)
    neg = jnp.float32(-jnp.inf)
    logits = jnp.where(lane < N_EXPERTS, logits, neg)
    m1 = jnp.max(logits, axis=-1, keepdims=True)
    i1 = jnp.min(jnp.where(logits == m1, lane, LANES), axis=-1, keepdims=True)
    rest = jnp.where(lane == i1, neg, logits)
    m2 = jnp.max(rest, axis=-1, keepdims=True)
    i2 = jnp.min(jnp.where(rest == m2, lane, LANES), axis=-1, keepdims=True)
    e2 = jnp.exp(m2 - m1)
    g1 = 1.0 / (1.0 + e2)
    g2 = e2 / (1.0 + e2)
    route = jnp.where(lane == ROUTE_LANE, g1, 0.0) + jnp.where(lane == ROUTE_LANE + 1, g2, 0.0)
    route = route + jnp.where(lane == ROUTE_LANE + 2, i1.astype(jnp.float32), 0.0)
    route = route + jnp.where(lane == ROUTE_LANE + 3, i2.astype(jnp.float32), 0.0)
    gates_ref[...] = route


def _rmsnorm(x, gain, router_w=None, tm=512):
    n, d = x.shape
    grid = (n // tm,)
    x_spec = pl.BlockSpec((tm, d), lambda i: (i, 0))
    g_spec = pl.BlockSpec((1, d), lambda i: (0, 0))
    o_spec = pl.BlockSpec((tm, d), lambda i: (i, 0))
    gain2 = gain.reshape(1, d)
    if router_w is None:
        return pl.pallas_call(
            _norm_kernel, grid=grid, in_specs=[x_spec, g_spec], out_specs=o_spec,
            out_shape=jax.ShapeDtypeStruct((n, d), MXU_DTYPE),
            compiler_params=_cparams("parallel"), name="rmsnorm")(x, gain2)
    wr = jnp.pad(router_w, ((0, 0), (0, LANES - router_w.shape[1])))
    return pl.pallas_call(
        _norm_router_kernel, grid=grid,
        in_specs=[x_spec, g_spec, pl.BlockSpec((d, LANES), lambda i: (0, 0))],
        out_specs=[o_spec, pl.BlockSpec((tm, LANES), lambda i: (i, 0))],
        out_shape=[jax.ShapeDtypeStruct((n, d), jnp.float32),
                   jax.ShapeDtypeStruct((n, LANES), jnp.float32)],
        compiler_params=_cparams("parallel"), name="rmsnorm_router")(x, gain2, wr)


def _proj_kernel(x_ref, w_ref, gain_ref, o_ref, *, norm_heads, scale):
    acc = _dot(x_ref[...], w_ref[...])
    tn = acc.shape[1]
    for h in range(tn // A_HEAD_DIM):
        y = acc[:, h * A_HEAD_DIM:(h + 1) * A_HEAD_DIM]
        if h < norm_heads:
            ms = jnp.mean(y * y, axis=-1, keepdims=True)
            y = y * lax.rsqrt(ms + EPS) * (gain_ref[...] * scale)
        o_ref[:, h * A_HEAD_DIM:(h + 1) * A_HEAD_DIM] = y.astype(o_ref.dtype)


def _project(x, w, gain=None, *, norm_heads=0, scale=1.0, out_dtype=None, tm=1024, tn=512):
    out_dtype = out_dtype or MXU_DTYPE
    n, d = x.shape
    f = w.shape[1]
    tn = min(tn, f)
    if gain is None:
        gain = jnp.ones((A_HEAD_DIM,), jnp.float32)
    return pl.pallas_call(
        functools.partial(_proj_kernel, norm_heads=norm_heads, scale=scale),
        grid=(n // tm, f // tn),
        in_specs=[pl.BlockSpec((tm, d), lambda i, j: (i, 0)),
                  pl.BlockSpec((d, tn), lambda i, j: (0, j)),
                  pl.BlockSpec((1, A_HEAD_DIM), lambda i, j: (0, 0))],
        out_specs=pl.BlockSpec((tm, tn), lambda i, j: (i, j)),
        out_shape=jax.ShapeDtypeStruct((n, f), out_dtype),
        compiler_params=_cparams("parallel", "parallel"), name="in_proj")(
            x, w, gain.reshape(1, A_HEAD_DIM))


def _outproj_kernel(a_ref, r_ref, wa_ref, wr_ref, res_ref, o_ref):
    o_ref[...] = res_ref[...] + (_dot(a_ref[...], wa_ref[...]) + _dot(r_ref[...], wr_ref[...]))


def _out_project(a, r, w, res, tm=1024, tn=512):
    n, ka = a.shape
    kr = r.shape[1]
    assert ka == kr
    d = w.shape[1]
    tn = min(tn, d)
    return pl.pallas_call(
        _outproj_kernel, grid=(n // tm, d // tn),
        in_specs=[pl.BlockSpec((tm, ka), lambda i, j: (i, 0)),
                  pl.BlockSpec((tm, kr), lambda i, j: (i, 0)),
                  pl.BlockSpec((ka, tn), lambda i, j: (0, j)),
                  pl.BlockSpec((kr, tn), lambda i, j: (1, j)),
                  pl.BlockSpec((tm, tn), lambda i, j: (i, j))],
        out_specs=pl.BlockSpec((tm, tn), lambda i, j: (i, j)),
        out_shape=jax.ShapeDtypeStruct((n, d), jnp.float32),
        compiler_params=_cparams("parallel", "parallel"), name="out_proj")(a, r, w, w, res)


def _swiglu_hidden(x, wg, wu):
    g = _dot(x, wg)
    u = _dot(x, wu)
    return g * jax.nn.sigmoid(g) * u


def _gu_kernel(x_ref, wg_ref, wu_ref, o_ref):
    o_ref[...] = _swiglu_hidden(x_ref[...], wg_ref[...], wu_ref[...]).astype(o_ref.dtype)


def _gate_up(x, wg, wu, tm=1024, tn=512):
    n, d = x.shape
    f = wg.shape[1]
    return pl.pallas_call(
        _gu_kernel, grid=(n // tm, f // tn),
        in_specs=[pl.BlockSpec((tm, d), lambda i, j: (i, 0)),
                  pl.BlockSpec((d, tn), lambda i, j: (0, j)),
                  pl.BlockSpec((d, tn), lambda i, j: (0, j))],
        out_specs=pl.BlockSpec((tm, tn), lambda i, j: (i, j)),
        out_shape=jax.ShapeDtypeStruct((n, f), MXU_DTYPE),
        compiler_params=_cparams("parallel", "parallel"), name="gate_up")(x, wg, wu)


def _down_kernel(x_ref, w_ref, res_ref, o_ref):
    @pl.when(pl.program_id(2) == 0)
    def _():
        o_ref[...] = res_ref[...]
    o_ref[...] += _dot(x_ref[...], w_ref[...])


def _down(x, w, res, tk, tm=512, tn=1024):
    n, kt = x.shape
    d = w.shape[1]
    return pl.pallas_call(
        _down_kernel, grid=(n // tm, d // tn, kt // tk),
        in_specs=[pl.BlockSpec((tm, tk), lambda i, j, k: (i, k)),
                  pl.BlockSpec((tk, tn), lambda i, j, k: (k, j)),
                  pl.BlockSpec((tm, tn), lambda i, j, k: (i, j))],
        out_specs=pl.BlockSpec((tm, tn), lambda i, j, k: (i, j)),
        out_shape=jax.ShapeDtypeStruct((n, d), jnp.float32),
        compiler_params=_cparams("parallel", "parallel", "arbitrary"), name="down_proj")(x, w, res)


def _route_tables(route, tm):
    n = route.shape[0]
    e = route[:, ROUTE_LANE + 2:ROUTE_LANE + 4].astype(jnp.int32).reshape(-1)
    onehot = (e[:, None] == jnp.arange(N_EXPERTS, dtype=jnp.int32)[None, :]).astype(jnp.int32)
    csum = jnp.cumsum(onehot, axis=0)
    rank = jnp.sum(csum * onehot, axis=1) - 1
    tiles_per = (csum[-1] + tm - 1) // tm
    tile_end = jnp.cumsum(tiles_per)
    row_start = (tile_end - tiles_per) * tm
    pos = jnp.sum(onehot * row_start[None, :], axis=1) + rank
    n_tiles = (2 * n) // tm + N_EXPERTS
    t = jnp.arange(n_tiles, dtype=jnp.int32)
    tile_expert = jnp.sum((t[:, None] >= tile_end[None, :]).astype(jnp.int32), axis=1)
    meta = jnp.concatenate([jnp.minimum(tile_expert, N_EXPERTS - 1), tile_end[-1:]]).astype(jnp.int32)
    src = jnp.zeros((n_tiles * tm,), jnp.int32).at[pos].set(
        jnp.arange(2 * n, dtype=jnp.int32) // 2, unique_indices=True)
    return src.reshape(n_tiles, 1, tm), pos.reshape(n, 2), meta


def _row_copy(src_hbm, row, dst, slot, r, sem):
    return pltpu.make_async_copy(src_hbm.at[pl.ds(row, 1)], dst.at[slot, pl.ds(r, 1)], sem)


def _gather_kernel(idx_ref, nxt_ref, x_hbm, o_ref, buf, sem):
    i = pl.program_id(0)
    tm = buf.shape[1]
    slot = lax.rem(i, 2)

    def issue(ref, s):
        def body(r, c):
            _row_copy(x_hbm, ref[0, 0, r], buf, s, r, sem.at[s]).start()
            return c
        lax.fori_loop(0, tm, body, 0)

    @pl.when(i == 0)
    def _():
        issue(idx_ref, 0)

    @pl.when(i + 1 < pl.num_programs(0))
    def _():
        issue(nxt_ref, 1 - slot)

    pltpu.make_async_copy(x_hbm.at[pl.ds(0, tm)], buf.at[slot], sem.at[slot]).wait()
    o_ref[...] = buf[slot].astype(o_ref.dtype)


def _moe_gather(x, src):
    n_tiles, _, tm = src.shape
    d = x.shape[1]
    idx_spec = pl.BlockSpec((1, 1, tm), lambda i: (i, 0, 0), memory_space=pltpu.SMEM)
    nxt_spec = pl.BlockSpec((1, 1, tm), lambda i: (jnp.minimum(i + 1, n_tiles - 1), 0, 0),
                            memory_space=pltpu.SMEM)
    return pl.pallas_call(
        _gather_kernel, grid=(n_tiles,),
        in_specs=[idx_spec, nxt_spec, pl.BlockSpec(memory_space=pl.ANY)],
        out_specs=pl.BlockSpec((tm, d), lambda i: (i, 0)),
        out_shape=jax.ShapeDtypeStruct((n_tiles * tm, d), MXU_DTYPE),
        scratch_shapes=[pltpu.VMEM((2, tm, d), x.dtype), pltpu.SemaphoreType.DMA((2,))],
        compiler_params=_cparams("arbitrary"), name="moe_gather")(src, src, x)


def _moe_gu_kernel(meta_ref, x_ref, wg_ref, wu_ref, o_ref):
    used = pl.program_id(1) < meta_ref[pl.num_programs(1)]

    @pl.when(used)
    def _():
        o_ref[...] = _swiglu_hidden(x_ref[...], wg_ref[0], wu_ref[0]).astype(o_ref.dtype)

    @pl.when(jnp.logical_not(used))
    def _():
        o_ref[...] = jnp.zeros_like(o_ref)


def _moe_gate_up(xs, wg, wu, meta, tm, tn=1024):
    p, d = xs.shape
    f = wg.shape[2]
    tn = min(tn, f)
    grid_spec = pltpu.PrefetchScalarGridSpec(
        num_scalar_prefetch=1, grid=(f // tn, p // tm),
        in_specs=[pl.BlockSpec((tm, d), lambda j, i, m: (i, 0)),
                  pl.BlockSpec((1, d, tn), lambda j, i, m: (m[i], 0, j)),
                  pl.BlockSpec((1, d, tn), lambda j, i, m: (m[i], 0, j))],
        out_specs=pl.BlockSpec((tm, tn), lambda j, i, m: (i, j)))
    return pl.pallas_call(
        _moe_gu_kernel, grid_spec=grid_spec,
        out_shape=jax.ShapeDtypeStruct((p, f), MXU_DTYPE),
        compiler_params=_cparams("parallel", "parallel"), name="moe_gate_up")(meta, xs, wg, wu)


def _moe_down_kernel(meta_ref, x_ref, w_ref, o_ref):
    used = pl.program_id(1) < meta_ref[pl.num_programs(1)]

    @pl.when(used)
    def _():
        o_ref[...] = _dot(x_ref[...], w_ref[0])

    @pl.when(jnp.logical_not(used))
    def _():
        o_ref[...] = jnp.zeros_like(o_ref)


def _moe_down(hid, wd, meta, tm, tn=512):
    p, f = hid.shape
    d = wd.shape[2]
    tn = min(tn, d)
    grid_spec = pltpu.PrefetchScalarGridSpec(
        num_scalar_prefetch=1, grid=(d // tn, p // tm),
        in_specs=[pl.BlockSpec((tm, f), lambda j, i, m: (i, 0)),
                  pl.BlockSpec((1, f, tn), lambda j, i, m: (m[i], 0, j))],
        out_specs=pl.BlockSpec((tm, tn), lambda j, i, m: (i, j)))
    return pl.pallas_call(
        _moe_down_kernel, grid_spec=grid_spec,
        out_shape=jax.ShapeDtypeStruct((p, d), jnp.float32),
        compiler_params=_cparams("parallel", "parallel"), name="moe_down")(meta, hid, wd)


def _combine_kernel(pa_ref, pb_ref, pa_nxt, pb_nxt, y_hbm, h_ref, route_ref, o_ref, bufa, bufb, sem):
    i = pl.program_id(0)
    tm = bufa.shape[1]
    slot = lax.rem(i, 2)

    def issue(pa, pb, s):
        def body(r, c):
            _row_copy(y_hbm, pa[0, 0, r], bufa, s, r, sem.at[0, s]).start()
            _row_copy(y_hbm, pb[0, 0, r], bufb, s, r, sem.at[1, s]).start()
            return c
        lax.fori_loop(0, tm, body, 0)

    @pl.when(i == 0)
    def _():
        issue(pa_ref, pb_ref, 0)

    @pl.when(i + 1 < pl.num_programs(0))
    def _():
        issue(pa_nxt, pb_nxt, 1 - slot)

    pltpu.make_async_copy(y_hbm.at[pl.ds(0, tm)], bufa.at[slot], sem.at[0, slot]).wait()
    pltpu.make_async_copy(y_hbm.at[pl.ds(0, tm)], bufb.at[slot], sem.at[1, slot]).wait()
    route = route_ref[...]
    ga = route[:, ROUTE_LANE:ROUTE_LANE + 1]
    gb = route[:, ROUTE_LANE + 1:ROUTE_LANE + 2]
    o_ref[...] = h_ref[...] + (ga * bufa[slot] + gb * bufb[slot])


def _moe_combine(y, pos, h, route, tm):
    n, d = h.shape
    nt = n // tm
    pa = pos[:, 0].reshape(nt, 1, tm)
    pb = pos[:, 1].reshape(nt, 1, tm)
    cur = pl.BlockSpec((1, 1, tm), lambda i: (i, 0, 0), memory_space=pltpu.SMEM)
    nxt = pl.BlockSpec((1, 1, tm), lambda i: (jnp.minimum(i + 1, nt - 1), 0, 0), memory_space=pltpu.SMEM)
    return pl.pallas_call(
        _combine_kernel, grid=(nt,),
        in_specs=[cur, cur, nxt, nxt, pl.BlockSpec(memory_space=pl.ANY),
                  pl.BlockSpec((tm, d), lambda i: (i, 0)),
                  pl.BlockSpec((tm, LANES), lambda i: (i, 0))],
        out_specs=pl.BlockSpec((tm, d), lambda i: (i, 0)),
        out_shape=jax.ShapeDtypeStruct((n, d), jnp.float32),
        scratch_shapes=[pltpu.VMEM((2, tm, d), jnp.float32), pltpu.VMEM((2, tm, d), jnp.float32),
                        pltpu.SemaphoreType.DMA((2, 2))],
        compiler_params=_cparams("arbitrary"), name="moe_combine")(pa, pb, pa, pb, y, h, route)


def _moe(h, norm_gain, router_w, wg, wu, wd):
    hn, route = _rmsnorm(h, norm_gain, router_w=router_w)
    src, pos, meta = _route_tables(route, MOE_TM)
    xs = _moe_gather(hn, src)
    hid = _moe_gate_up(xs, wg.astype(MXU_DTYPE), wu.astype(MXU_DTYPE), meta, MOE_TM)
    y = _moe_down(hid, wd.astype(MXU_DTYPE), meta, MOE_TM)
    return _moe_combine(y, pos, h, route, COMBINE_TM)


def _sortable(x):
    x = jnp.where(x == 0.0, 0.0, x)
    i = lax.bitcast_convert_type(x, jnp.int32)
    return i ^ ((i >> 31) & jnp.int32(0x7FFFFFFF))


def _dsa_kernel(q_ref, kv_ref, iq_ref, ikk_ref, ikq_ref, nb_ref, o_ref,
                iklo_ref, ikhi_ref, vt_ref, wt_ref, skey_ref, m_ref, l_ref, a_ref, acc_ref,
                s_ref, p_ref, *, n_top):
    tq = DSA_TQ
    half_keys = tq // 2
    qi = pl.program_id(1)
    f32 = jnp.float32
    nkb = skey_ref.shape[0]

    @pl.when(qi == 0)
    def _():
        ik = ikk_ref[...]
        lane = lax.broadcasted_iota(jnp.int32, ik.shape, 1)
        iklo_ref[...] = jnp.where(lane < IDX_DIM, ik, 0.0).astype(MXU_DTYPE)
        shifted = pltpu.roll(ik, IDX_DIM, axis=1)
        ikhi_ref[...] = jnp.where(lane >= IDX_DIM, shifted, 0.0).astype(MXU_DTYPE)
        for kb in range(nkb):
            for g in range(A_KV_HEADS):
                v = kv_ref[kb * tq:(kb + 1) * tq, A_KV + g * A_HEAD_DIM:A_KV + (g + 1) * A_HEAD_DIM]
                vt_ref[g, kb] = v.astype(f32).T.astype(MXU_DTYPE)

    wt_ref[...] = ikq_ref[...].T

    key_iota = lax.broadcasted_iota(jnp.int32, (half_keys, tq), 0)
    t_chunk = (qi * tq + lax.broadcasted_iota(jnp.int32, (half_keys, tq), 1)) // CHUNK

    def score_block(kb, carry):
        for half in range(2):
            r = pl.multiple_of(kb * tq + half * half_keys, half_keys)
            klo = iklo_ref[pl.ds(r, half_keys), :]
            khi = ikhi_ref[pl.ds(r, half_keys), :]
            acc = jnp.zeros((half_keys, tq), f32)
            for p in range(IDX_HEADS // 2):
                rhs = iq_ref[:, p * LANES:(p + 1) * LANES]
                for lhs, h in ((klo, 2 * p), (khi, 2 * p + 1)):
                    hs = lax.dot_general(lhs, rhs, _NT, preferred_element_type=f32)
                    acc = acc + jnp.maximum(hs, 0.0) * wt_ref[IDX_DIM + h:IDX_DIM + h + 1, :]
            key = _sortable(acc)
            s_chunk = (r + key_iota) // CHUNK
            rows = slice(half * half_keys, (half + 1) * half_keys)
            skey_ref[kb, rows, :] = jnp.where(s_chunk <= t_chunk, key, INT_MIN)
        return carry

    lax.fori_loop(0, qi + 1, score_block, 0)

    n_groups = tq // SUBLANES
    n_acc = 4
    sub_iota = lax.broadcasted_iota(jnp.int32, (SUBLANES, tq), 0)

    def rows8(x):
        return jnp.broadcast_to(x, (SUBLANES, tq))

    def count_where(pred):
        def body(kb, accs):
            accs = list(accs)
            for g in range(n_groups):
                keys = skey_ref[kb, g * SUBLANES:(g + 1) * SUBLANES, :]
                hit = jnp.where(pred(keys, kb * tq + g * SUBLANES), 1.0, 0.0)
                accs[g % n_acc] = accs[g % n_acc] + hit
            return tuple(accs)

        accs = lax.fori_loop(0, qi + 1, body,
                             tuple(jnp.zeros((SUBLANES, tq), f32) for _ in range(n_acc)))
        return jnp.sum((accs[0] + accs[1]) + (accs[2] + accs[3]), axis=0, keepdims=True)

    def bit_step(it, tu):
        bit = jnp.left_shift(jnp.int32(1), 31 - it)
        cand = rows8((tu | bit) ^ INT_MIN)
        cnt = count_where(lambda keys, k0: keys >= cand)
        return jnp.where(cnt >= n_top, tu | bit, tu)

    tu = lax.fori_loop(0, 32, bit_step, jnp.zeros((1, tq), jnp.int32))
    thr = rows8(tu ^ INT_MIN)

    n_gt = count_where(lambda keys, k0: keys > thr)
    n_eq = count_where(lambda keys, k0: keys == thr)
    need = n_top - n_gt
    n_bits = max(1, int(math.ceil(math.log2(nkb * tq + 1))))

    def tie_search():
        def jstep(it, jb):
            bit = jnp.left_shift(jnp.int32(1), n_bits - 1 - it)
            bound = rows8(jb | bit)
            cnt = count_where(lambda keys, k0: (keys == thr) & ((k0 + sub_iota) < bound))
            return jnp.where(cnt <= need, jb | bit, jb)

        return lax.fori_loop(0, n_bits, jstep, jnp.zeros((1, tq), jnp.int32))

    jb = lax.cond(jnp.max(n_eq - need) > 0.0, tie_search,
                  lambda: jnp.full((1, tq), 2 ** 30, jnp.int32))
    bound = rows8(jb)

    def mask_block(kb, carry):
        for g in range(n_groups):
            rows = slice(g * SUBLANES, (g + 1) * SUBLANES)
            keys = skey_ref[kb, rows, :]
            tie = (keys == thr) & ((kb * tq + g * SUBLANES + sub_iota) < bound)
            sel = ((keys > thr) | tie) & (keys != INT_MIN)
            skey_ref[kb, rows, :] = lax.bitcast_convert_type(
                jnp.where(sel, 0.0, MASKED).astype(f32), jnp.int32)
        return carry

    lax.fori_loop(0, qi + 1, mask_block, 0)

    m_ref[...] = jnp.full(m_ref.shape, MASKED, f32)
    l_ref[...] = jnp.zeros_like(l_ref)
    acc_ref[...] = jnp.zeros_like(acc_ref)
    grp = A_HEADS // A_KV_HEADS

    def attend(kb, near):
        r = pl.multiple_of(kb * tq, tq)
        mb = lax.bitcast_convert_type(skey_ref[kb], f32)
        for h in range(A_HEADS):
            g = h // grp
            kblk = kv_ref[pl.ds(r, tq), g * A_HEAD_DIM:(g + 1) * A_HEAD_DIM]
            qh = q_ref[:, h * A_HEAD_DIM:(h + 1) * A_HEAD_DIM]
            s = lax.dot_general(kblk, qh, _NT, preferred_element_type=f32) + mb
            if near is not None:
                s = s + nb_ref[h, near * tq:(near + 1) * tq, :]
            s_ref[h] = s
        for h in range(A_HEADS):
            s = s_ref[h]
            m_prev = m_ref[h]
            m_new = jnp.maximum(m_prev, jnp.max(s, axis=0, keepdims=True))
            alpha = jnp.exp2(m_prev - m_new)
            p = jnp.exp2(s - m_new)
            l_ref[h] = alpha * l_ref[h] + jnp.sum(p, axis=0, keepdims=True)
            m_ref[h] = m_new
            a_ref[h] = alpha
            p_ref[h] = p.astype(MXU_DTYPE)
        for h in range(A_HEADS):
            vt = vt_ref[h // grp, kb]
            acc_ref[h] = a_ref[h] * acc_ref[h] + _dot(vt, p_ref[h])

    def far_block(kb, carry):
        attend(kb, None)
        return carry

    lax.fori_loop(0, jnp.maximum(qi - 1, 0), far_block, 0)

    @pl.when(qi >= 1)
    def _():
        attend(qi - 1, 0)

    attend(qi, 1)

    for h in range(A_HEADS):
        o = (acc_ref[h] / l_ref[h]).T
        o_ref[:, h * A_HEAD_DIM:(h + 1) * A_HEAD_DIM] = o.astype(o_ref.dtype)


def _t5_bucket(rel):
    half = REL_BUCKETS // 2
    max_exact = half // 2
    ret = jnp.where(rel > 0, half, 0)
    n = jnp.abs(rel)
    nf = jnp.maximum(n, 1).astype(jnp.float32)
    large = max_exact + (jnp.log(nf / max_exact) / math.log(REL_MAX_DIST / max_exact)
                         * (half - max_exact)).astype(jnp.int32)
    large = jnp.minimum(large, half - 1)
    return ret + jnp.where(n < max_exact, n, large)


def _near_bias(rel_bias):
    tq = DSA_TQ
    c = jnp.arange(2 * tq, dtype=jnp.int32)[:, None]
    i = jnp.arange(tq, dtype=jnp.int32)[None, :]
    rel = c - tq - i
    far = rel_bias[_t5_bucket(jnp.full((1,), -REL_MAX_DIST, jnp.int32))]
    onehot = (_t5_bucket(rel)[None] == jnp.arange(REL_BUCKETS, dtype=jnp.int32)[:, None, None])
    table = (rel_bias - far).T * LOG2E
    return jnp.einsum('hb,bci->hci', table, onehot.astype(jnp.float32),
                      precision=lax.Precision.HIGHEST)


def _dsa(q, kv, iq, ikw, near_bias, batch, seq):
    tq = DSA_TQ
    nq = seq // tq
    n_top = min(TOPK_MAX, seq // 4)
    return pl.pallas_call(
        functools.partial(_dsa_kernel, n_top=n_top),
        grid=(batch, nq),
        in_specs=[pl.BlockSpec((tq, A_Q), lambda b, i: (b * nq + i, 0)),
                  pl.BlockSpec((seq, 2 * A_KV), lambda b, i: (b, 0)),
                  pl.BlockSpec((tq, IDX_Q), lambda b, i: (b * nq + i, 0)),
                  pl.BlockSpec((seq, LANES), lambda b, i: (b, 0)),
                  pl.BlockSpec((tq, LANES), lambda b, i: (b * nq + i, 0)),
                  pl.BlockSpec((A_HEADS, 2 * tq, tq), lambda b, i: (0, 0, 0))],
        out_specs=pl.BlockSpec((tq, A_Q), lambda b, i: (b * nq + i, 0)),
        out_shape=jax.ShapeDtypeStruct((batch * seq, A_Q), MXU_DTYPE),
        scratch_shapes=[pltpu.VMEM((seq, LANES), MXU_DTYPE),
                        pltpu.VMEM((seq, LANES), MXU_DTYPE),
                        pltpu.VMEM((A_KV_HEADS, nq, A_HEAD_DIM, tq), MXU_DTYPE),
                        pltpu.VMEM((LANES, tq), jnp.float32),
                        pltpu.VMEM((nq, tq, tq), jnp.int32),
                        pltpu.VMEM((A_HEADS, 1, tq), jnp.float32),
                        pltpu.VMEM((A_HEADS, 1, tq), jnp.float32),
                        pltpu.VMEM((A_HEADS, 1, tq), jnp.float32),
                        pltpu.VMEM((A_HEADS, A_HEAD_DIM, tq), jnp.float32),
                        pltpu.VMEM((A_HEADS, tq, tq), jnp.float32),
                        pltpu.VMEM((A_HEADS, tq, tq), MXU_DTYPE)],
        compiler_params=_cparams("arbitrary", "arbitrary"), name="dsa")(
            q, kv, iq, ikw, ikw, near_bias)


def _ret_kernel(qk_ref, vg_ref, cos_ref, sin_ref, gn_ref, o_ref, state_ref, dm_ref):
    c_len = RET_CHUNK
    f32 = jnp.float32
    hd = R_HEAD_DIM
    half = hd // 2
    log_g = [math.log(1.0 - 2.0 ** (-5.0 - h)) for h in range(R_HEADS)]

    @pl.when(pl.program_id(1) == 0)
    def _():
        state_ref[...] = jnp.zeros_like(state_ref)
        i = lax.broadcasted_iota(jnp.int32, (c_len, c_len), 0)
        j = lax.broadcasted_iota(jnp.int32, (c_len, c_len), 1)
        diff = (i - j).astype(f32)
        for h in range(R_HEADS):
            dm_ref[h] = jnp.where(diff >= 0, jnp.exp(log_g[h] * jnp.maximum(diff, 0.0)), 0.0)

    row = lax.broadcasted_iota(jnp.int32, (c_len, hd), 0).astype(f32)
    cos = cos_ref[...]
    sin = sin_ref[...]

    def rot(x):
        x1, x2 = x[:, :half], x[:, half:]
        return jnp.concatenate([x1 * cos - x2 * sin, x1 * sin + x2 * cos], axis=1)

    for h in range(R_HEADS):
        cols = slice(h * hd, (h + 1) * hd)
        qf = rot(qk_ref[:, cols].astype(f32))
        kf = rot(qk_ref[:, R_W + h * hd:R_W + (h + 1) * hd].astype(f32)) * (hd ** -0.5)
        v = vg_ref[:, cols]
        att = lax.dot_general(qf.astype(MXU_DTYPE), kf.astype(MXU_DTYPE), _NT,
                              preferred_element_type=f32) * dm_ref[h]
        xi = jnp.exp(log_g[h] * (row + 1.0))
        zeta = jnp.exp(log_g[h] * (c_len - 1.0 - row))
        state = state_ref[h]
        o = _dot(att.astype(MXU_DTYPE), v) + _dot((qf * xi).astype(MXU_DTYPE), state.astype(MXU_DTYPE))
        state_ref[h] = state * math.exp(log_g[h] * c_len) + lax.dot_general(
            (kf * zeta).astype(MXU_DTYPE), v, _TN, preferred_element_type=f32)
        mu = jnp.mean(o, axis=-1, keepdims=True)
        d = o - mu
        var = jnp.mean(d * d, axis=-1, keepdims=True)
        y = d * lax.rsqrt(var + GN_EPS) * gn_ref[:, cols]
        g = vg_ref[:, R_W + h * hd:R_W + (h + 1) * hd].astype(f32)
        o_ref[:, cols] = (g * jax.nn.sigmoid(g) * y).astype(o_ref.dtype)


def _rope_tables(seq):
    half = R_HEAD_DIM // 2
    inv = ROPE_BASE ** (-jnp.arange(half, dtype=jnp.float32) / half)
    ang = jnp.arange(seq, dtype=jnp.float32)[:, None] * inv[None, :]
    return jnp.cos(ang), jnp.sin(ang)


def _retention(rqk, rvg, cos, sin, gn_gain, batch, seq):
    c_len = RET_CHUNK
    nc = seq // c_len
    half = R_HEAD_DIM // 2
    return pl.pallas_call(
        _ret_kernel, grid=(batch, nc),
        in_specs=[pl.BlockSpec((c_len, 2 * R_W), lambda b, c: (b * nc + c, 0)),
                  pl.BlockSpec((c_len, 2 * R_W), lambda b, c: (b * nc + c, 0)),
                  pl.BlockSpec((c_len, half), lambda b, c: (c, 0)),
                  pl.BlockSpec((c_len, half), lambda b, c: (c, 0)),
                  pl.BlockSpec((1, R_W), lambda b, c: (0, 0))],
        out_specs=pl.BlockSpec((c_len, R_W), lambda b, c: (b * nc + c, 0)),
        out_shape=jax.ShapeDtypeStruct((batch * seq, R_W), MXU_DTYPE),
        scratch_shapes=[pltpu.VMEM((R_HEADS, R_HEAD_DIM, R_HEAD_DIM), jnp.float32),
                        pltpu.VMEM((R_HEADS, c_len, c_len), jnp.float32)],
        compiler_params=_cparams("arbitrary", "arbitrary"), name="retention")(
            rqk, rvg, cos, sin, gn_gain.reshape(1, R_W))


def _mixer(h, norm_gain, w_in, q_gain, k_gain, gn_gain, w_out, near_bias, cos, sin, batch, seq):
    xn = _rmsnorm(h, norm_gain)
    w = w_in.astype(MXU_DTYPE)
    c0 = A_Q
    c1 = c0 + 2 * A_KV
    c2 = c1 + IDX_Q
    c3 = c2 + IDX_DIM + IDX_HEADS
    c4 = c3 + 2 * R_W
    q = _project(xn, w[:, :c0], q_gain, norm_heads=4, scale=A_HEAD_DIM ** -0.5 * LOG2E)
    kv = _project(xn, w[:, c0:c1], k_gain, norm_heads=A_KV // A_HEAD_DIM)
    iq = _project(xn, w[:, c1:c2])
    w_ikw = jnp.pad(w[:, c2:c3], ((0, 0), (0, LANES - (c3 - c2))))
    ikw = _project(xn, w_ikw, out_dtype=jnp.float32)
    rqk = _project(xn, w[:, c3:c4])
    rvg = _project(xn, w[:, c4:])
    a_out = _dsa(q, kv, iq, ikw, near_bias, batch, seq)
    r_out = _retention(rqk, rvg, cos, sin, gn_gain, batch, seq)
    return _out_project(a_out, r_out, w_out.astype(MXU_DTYPE), h)


def kernel(x, rel_bias, norm_mix, w_in, q_gain, k_gain, ret_gain, w_out, norm_ffn,
           ffn_w_gate, ffn_w_up, ffn_w_down, moe_router, moe_w_gate, moe_w_up, moe_w_down):
    batch, seq, d = x.shape
    depth = norm_mix.shape[0]
    h = x.reshape(batch * seq, d)
    near_bias = _near_bias(rel_bias)
    cos, sin = _rope_tables(seq)
    for l in range(depth):
        h = _mixer(h, norm_mix[l], w_in[l], q_gain[l], k_gain[l], ret_gain[l], w_out[l],
                   near_bias, cos, sin, batch, seq)
        j = l // 2
        if l % 2 == 0:
            hn = _rmsnorm(h, norm_ffn[l])
            hid = _gate_up(hn, ffn_w_gate[j].astype(MXU_DTYPE), ffn_w_up[j].astype(MXU_DTYPE))
            f = ffn_w_down.shape[1]
            h = _down(hid, ffn_w_down[j].astype(MXU_DTYPE), h, tk=f // 2)
        else:
            h = _moe(h, norm_ffn[l], moe_router[j], moe_w_gate[j], moe_w_up[j], moe_w_down[j])
    return h.reshape(batch, seq, d)
```

```python
import functools
import math

import jax
import jax.numpy as jnp
import numpy as np
from jax import lax
from jax.experimental import pallas as pl
from jax.experimental.pallas import tpu as pltpu

CHUNK = 64
A_HEADS = 8
A_KV_HEADS = 2
A_HEAD_DIM = 128
IDX_HEADS = 16
IDX_DIM = 64
TOPK_MAX = 256
REL_BUCKETS = 32
REL_MAX_DIST = 128
R_HEADS = 4
R_HEAD_DIM = 256
ROPE_BASE = 10000.0
A_Q = A_HEADS * A_HEAD_DIM
A_KV = A_KV_HEADS * A_HEAD_DIM
IDX_Q = IDX_HEADS * IDX_DIM
R_W = R_HEADS * R_HEAD_DIM
N_EXPERTS = 8
EPS = 1e-6
GN_EPS = 1e-5

LANES = 128
SUBLANES = 8
PACKED_ROWS = 16
HALF16 = 1 << 15
LOG2E = math.log2(math.e)
VMEM_LIMIT_BYTES = 56 * 1024 * 1024
MXU_DTYPE = jnp.bfloat16
DSA_TQ = 256
RET_CHUNK = 256
MOE_TM = 512
COMBINE_TM = 256
ROUTE_LANE = 8
INT_MIN = -2147483648
MASKED = -1e30

_NT = (((1,), (1,)), ((), ()))
_TN = (((0,), (0,)), ((), ()))


def _cparams(*sem):
    return pltpu.CompilerParams(dimension_semantics=sem, vmem_limit_bytes=VMEM_LIMIT_BYTES)


def _dot(a, b):
    return jnp.dot(a, b, preferred_element_type=jnp.float32)


def _rms(x_ref, g_ref):
    x = x_ref[...]
    ms = jnp.mean(x * x, axis=-1, keepdims=True)
    return x * lax.rsqrt(ms + EPS) * g_ref[...]


def _norm_router_kernel(x_ref, g_ref, wr_ref, o_ref, gates_ref):
    xn = _rms(x_ref, g_ref)
    o_ref[...] = xn.astype(o_ref.dtype)
    x_hi = xn.astype(jnp.bfloat16)
    x_lo = (xn - x_hi.astype(jnp.float32)).astype(jnp.bfloat16)
    w = wr_ref[...]
    w_hi = w.astype(jnp.bfloat16)
    w_lo = (w - w_hi.astype(jnp.float32)).astype(jnp.bfloat16)
    logits = _dot(x_hi, w_hi) + (_dot(x_lo, w_hi) + _dot(x_hi, w_lo))
    lane = lax.broadcasted_iota(jnp.int32, logits.shape, 1)
    neg = jnp.float32(-jnp.inf)
    logits = jnp.where(lane < N_EXPERTS, logits, neg)
    m1 = jnp.max(logits, axis=-1, keepdims=True)
    i1 = jnp.min(jnp.where(logits == m1, lane, LANES), axis=-1, keepdims=True)
    rest = jnp.where(lane == i1, neg, logits)
    m2 = jnp.max(rest, axis=-1, keepdims=True)
    i2 = jnp.min(jnp.where(rest == m2, lane, LANES), axis=-1, keepdims=True)
    e2 = jnp.exp(m2 - m1)
    g1 = 1.0 / (1.0 + e2)
    g2 = e2 / (1.0 + e2)
    route = jnp.where(lane == ROUTE_LANE, g1, 0.0) + jnp.where(lane == ROUTE_LANE + 1, g2, 0.0)
    route = route + jnp.where(lane == ROUTE_LANE + 2, i1.astype(jnp.float32), 0.0)
    route = route + jnp.where(lane == ROUTE_LANE + 3, i2.astype(jnp.float32), 0.0)
    gates_ref[...] = route


def _rmsnorm_router(x, gain, router_w, tm=512):
    n, d = x.shape
    wr = jnp.pad(router_w, ((0, 0), (0, LANES - router_w.shape[1])))
    return pl.pallas_call(
        _norm_router_kernel, grid=(n // tm,),
        in_specs=[pl.BlockSpec((tm, d), lambda i: (i, 0)),
                  pl.BlockSpec((1, d), lambda i: (0, 0)),
                  pl.BlockSpec((d, LANES), lambda i: (0, 0))],
        out_specs=[pl.BlockSpec((tm, d), lambda i: (i, 0)),
                   pl.BlockSpec((tm, LANES), lambda i: (i, 0))],
        out_shape=[jax.ShapeDtypeStruct((n, d), jnp.float32),
                   jax.ShapeDtypeStruct((n, LANES), jnp.float32)],
        compiler_params=_cparams("parallel"), name="rmsnorm_router")(x, gain.reshape(1, d), wr)


PROJ_TN = 512
COL_RQK = 0
COL_RVG = COL_RQK + 2 * R_W
COL_Q = COL_RVG + 2 * R_W
COL_IQ = COL_Q + A_Q
COL_KV = COL_IQ + IDX_Q
COL_IKW = COL_KV + 2 * A_KV
PROJ_WIDTH = COL_IKW + PROJ_TN


def _proj_kernel(x_ref, g_ref, w_ref, hg_ref, o_ref, ikw_ref, xn_ref):
    j = pl.program_id(1)

    @pl.when(j == 0)
    def _():
        xn_ref[...] = _rms(x_ref, g_ref).astype(xn_ref.dtype)

    acc = _dot(xn_ref[...], w_ref[...])
    heads = PROJ_TN // A_HEAD_DIM

    def store(norm_heads, gain_row):
        for h in range(heads):
            y = acc[:, h * A_HEAD_DIM:(h + 1) * A_HEAD_DIM]
            if h < norm_heads:
                ms = jnp.mean(y * y, axis=-1, keepdims=True)
                y = y * lax.rsqrt(ms + EPS) * hg_ref[gain_row:gain_row + 1, :]
            o_ref[:, h * A_HEAD_DIM:(h + 1) * A_HEAD_DIM] = y.astype(o_ref.dtype)

    is_q = jnp.logical_and(j >= COL_Q // PROJ_TN, j < COL_IQ // PROJ_TN)
    is_k = j == COL_KV // PROJ_TN

    @pl.when(is_q)
    def _():
        store(heads, 0)

    @pl.when(is_k)
    def _():
        store(A_KV // A_HEAD_DIM, 1)

    @pl.when(jnp.logical_not(jnp.logical_or(is_q, is_k)))
    def _():
        store(0, 0)

    @pl.when(j == COL_IKW // PROJ_TN)
    def _():
        ikw_ref[...] = acc[:, :LANES]


def _project(x, norm_gain, w_all, head_gains, tm=1024):
    n, d = x.shape
    return pl.pallas_call(
        _proj_kernel, grid=(n // tm, PROJ_WIDTH // PROJ_TN),
        in_specs=[pl.BlockSpec((tm, d), lambda i, j: (i, 0)),
                  pl.BlockSpec((1, d), lambda i, j: (0, 0)),
                  pl.BlockSpec((d, PROJ_TN), lambda i, j: (0, j)),
                  pl.BlockSpec((2, A_HEAD_DIM), lambda i, j: (0, 0))],
        out_specs=[pl.BlockSpec((tm, PROJ_TN), lambda i, j: (i, j)),
                   pl.BlockSpec((tm, LANES), lambda i, j: (i, 0))],
        out_shape=[jax.ShapeDtypeStruct((n, PROJ_WIDTH), MXU_DTYPE),
                   jax.ShapeDtypeStruct((n, LANES), jnp.float32)],
        scratch_shapes=[pltpu.VMEM((tm, d), MXU_DTYPE)],
        compiler_params=_cparams("parallel", "arbitrary"), name="in_proj")(
            x, norm_gain.reshape(1, d), w_all, head_gains)


def _merged_in_weights(w_in):
    c0 = A_Q
    c1 = c0 + 2 * A_KV
    c2 = c1 + IDX_Q
    c3 = c2 + IDX_DIM + IDX_HEADS
    c4 = c3 + 2 * R_W
    ikw = jnp.pad(w_in[:, c2:c3], ((0, 0), (0, PROJ_TN - (c3 - c2))))
    parts = [w_in[:, c3:c4], w_in[:, c4:], w_in[:, :c0], w_in[:, c1:c2], w_in[:, c0:c1], ikw]
    return jnp.concatenate(parts, axis=1).astype(MXU_DTYPE)


def _outproj_kernel(a_ref, r_ref, wa_ref, wr_ref, res_ref, o_ref):
    o_ref[...] = res_ref[...] + (_dot(a_ref[...], wa_ref[...]) + _dot(r_ref[...], wr_ref[...]))


def _out_project(a, r, w, res, tm=1024, tn=512):
    n, ka = a.shape
    kr = r.shape[1]
    assert ka == kr
    d = w.shape[1]
    tn = min(tn, d)
    return pl.pallas_call(
        _outproj_kernel, grid=(n // tm, d // tn),
        in_specs=[pl.BlockSpec((tm, ka), lambda i, j: (i, 0)),
                  pl.BlockSpec((tm, kr), lambda i, j: (i, 0)),
                  pl.BlockSpec((ka, tn), lambda i, j: (0, j)),
                  pl.BlockSpec((kr, tn), lambda i, j: (1, j)),
                  pl.BlockSpec((tm, tn), lambda i, j: (i, j))],
        out_specs=pl.BlockSpec((tm, tn), lambda i, j: (i, j)),
        out_shape=jax.ShapeDtypeStruct((n, d), jnp.float32),
        compiler_params=_cparams("parallel", "parallel"), name="out_proj")(a, r, w, w, res)


def _swiglu_hidden(x, wg, wu):
    g = _dot(x, wg)
    u = _dot(x, wu)
    return g * jax.nn.sigmoid(g) * u


def _gu_kernel(x_ref, g_ref, wg_ref, wu_ref, o_ref, xn_ref):
    @pl.when(pl.program_id(1) == 0)
    def _():
        xn_ref[...] = _rms(x_ref, g_ref).astype(xn_ref.dtype)

    o_ref[...] = _swiglu_hidden(xn_ref[...], wg_ref[...], wu_ref[...]).astype(o_ref.dtype)


def _gate_up(x, norm_gain, wg, wu, tm=1024, tn=512):
    n, d = x.shape
    f = wg.shape[1]
    return pl.pallas_call(
        _gu_kernel, grid=(n // tm, f // tn),
        in_specs=[pl.BlockSpec((tm, d), lambda i, j: (i, 0)),
                  pl.BlockSpec((1, d), lambda i, j: (0, 0)),
                  pl.BlockSpec((d, tn), lambda i, j: (0, j)),
                  pl.BlockSpec((d, tn), lambda i, j: (0, j))],
        out_specs=pl.BlockSpec((tm, tn), lambda i, j: (i, j)),
        out_shape=jax.ShapeDtypeStruct((n, f), MXU_DTYPE),
        scratch_shapes=[pltpu.VMEM((tm, d), MXU_DTYPE)],
        compiler_params=_cparams("parallel", "arbitrary"), name="gate_up")(
            x, norm_gain.reshape(1, d), wg, wu)


def _down_kernel(x_ref, w_ref, res_ref, o_ref):
    @pl.when(pl.program_id(2) == 0)
    def _():
        o_ref[...] = res_ref[...]
    o_ref[...] += _dot(x_ref[...], w_ref[...])


def _down(x, w, res, tk, tm=512, tn=1024):
    n, kt = x.shape
    d = w.shape[1]
    return pl.pallas_call(
        _down_kernel, grid=(n // tm, d // tn, kt // tk),
        in_specs=[pl.BlockSpec((tm, tk), lambda i, j, k: (i, k)),
                  pl.BlockSpec((tk, tn), lambda i, j, k: (k, j)),
                  pl.BlockSpec((tm, tn), lambda i, j, k: (i, j))],
        out_specs=pl.BlockSpec((tm, tn), lambda i, j, k: (i, j)),
        out_shape=jax.ShapeDtypeStruct((n, d), jnp.float32),
        compiler_params=_cparams("parallel", "parallel", "arbitrary"), name="down_proj")(x, w, res)


def _route_tables(route, tm):
    n = route.shape[0]
    e = route[:, ROUTE_LANE + 2:ROUTE_LANE + 4].astype(jnp.int32).reshape(-1)
    onehot = (e[:, None] == jnp.arange(N_EXPERTS, dtype=jnp.int32)[None, :]).astype(jnp.int32)
    csum = jnp.cumsum(onehot, axis=0)
    rank = jnp.sum(csum * onehot, axis=1) - 1
    tiles_per = (csum[-1] + tm - 1) // tm
    tile_end = jnp.cumsum(tiles_per)
    row_start = (tile_end - tiles_per) * tm
    pos = jnp.sum(onehot * row_start[None, :], axis=1) + rank
    n_tiles = (2 * n) // tm + N_EXPERTS
    t = jnp.arange(n_tiles, dtype=jnp.int32)
    tile_expert = jnp.sum((t[:, None] >= tile_end[None, :]).astype(jnp.int32), axis=1)
    meta = jnp.concatenate([jnp.minimum(tile_expert, N_EXPERTS - 1), tile_end[-1:]]).astype(jnp.int32)
    src = jnp.zeros((n_tiles * tm,), jnp.int32).at[pos].set(
        jnp.arange(2 * n, dtype=jnp.int32) // 2, unique_indices=True)
    return src.reshape(n_tiles, 1, tm), pos.reshape(n, 2), meta


def _row_copy(src_hbm, row, dst, slot, r, sem):
    return pltpu.make_async_copy(src_hbm.at[pl.ds(row, 1)], dst.at[slot, pl.ds(r, 1)], sem)


def _gather_kernel(idx_ref, nxt_ref, x_hbm, o_ref, buf, sem):
    i = pl.program_id(0)
    tm = buf.shape[1]
    slot = lax.rem(i, 2)

    def issue(ref, s):
        def body(r, c):
            _row_copy(x_hbm, ref[0, 0, r], buf, s, r, sem.at[s]).start()
            return c
        lax.fori_loop(0, tm, body, 0)

    @pl.when(i == 0)
    def _():
        issue(idx_ref, 0)

    @pl.when(i + 1 < pl.num_programs(0))
    def _():
        issue(nxt_ref, 1 - slot)

    pltpu.make_async_copy(x_hbm.at[pl.ds(0, tm)], buf.at[slot], sem.at[slot]).wait()
    o_ref[...] = buf[slot].astype(o_ref.dtype)


def _moe_gather(x, src):
    n_tiles, _, tm = src.shape
    d = x.shape[1]
    idx_spec = pl.BlockSpec((1, 1, tm), lambda i: (i, 0, 0), memory_space=pltpu.SMEM)
    nxt_spec = pl.BlockSpec((1, 1, tm), lambda i: (jnp.minimum(i + 1, n_tiles - 1), 0, 0),
                            memory_space=pltpu.SMEM)
    return pl.pallas_call(
        _gather_kernel, grid=(n_tiles,),
        in_specs=[idx_spec, nxt_spec, pl.BlockSpec(memory_space=pl.ANY)],
        out_specs=pl.BlockSpec((tm, d), lambda i: (i, 0)),
        out_shape=jax.ShapeDtypeStruct((n_tiles * tm, d), MXU_DTYPE),
        scratch_shapes=[pltpu.VMEM((2, tm, d), x.dtype), pltpu.SemaphoreType.DMA((2,))],
        compiler_params=_cparams("arbitrary"), name="moe_gather")(src, src, x)


def _moe_gu_kernel(meta_ref, x_ref, wg_ref, wu_ref, o_ref):
    used = pl.program_id(1) < meta_ref[pl.num_programs(1)]

    @pl.when(used)
    def _():
        o_ref[...] = _swiglu_hidden(x_ref[...], wg_ref[0], wu_ref[0]).astype(o_ref.dtype)

    @pl.when(jnp.logical_not(used))
    def _():
        o_ref[...] = jnp.zeros_like(o_ref)


def _moe_gate_up(xs, wg, wu, meta, tm, tn=1024):
    p, d = xs.shape
    f = wg.shape[2]
    tn = min(tn, f)
    grid_spec = pltpu.PrefetchScalarGridSpec(
        num_scalar_prefetch=1, grid=(f // tn, p // tm),
        in_specs=[pl.BlockSpec((tm, d), lambda j, i, m: (i, 0)),
                  pl.BlockSpec((1, d, tn), lambda j, i, m: (m[i], 0, j)),
                  pl.BlockSpec((1, d, tn), lambda j, i, m: (m[i], 0, j))],
        out_specs=pl.BlockSpec((tm, tn), lambda j, i, m: (i, j)))
    return pl.pallas_call(
        _moe_gu_kernel, grid_spec=grid_spec,
        out_shape=jax.ShapeDtypeStruct((p, f), MXU_DTYPE),
        compiler_params=_cparams("parallel", "parallel"), name="moe_gate_up")(meta, xs, wg, wu)


def _moe_down_kernel(meta_ref, x_ref, w_ref, o_ref):
    used = pl.program_id(1) < meta_ref[pl.num_programs(1)]

    @pl.when(used)
    def _():
        o_ref[...] = _dot(x_ref[...], w_ref[0])

    @pl.when(jnp.logical_not(used))
    def _():
        o_ref[...] = jnp.zeros_like(o_ref)


def _moe_down(hid, wd, meta, tm, tn=512):
    p, f = hid.shape
    d = wd.shape[2]
    tn = min(tn, d)
    grid_spec = pltpu.PrefetchScalarGridSpec(
        num_scalar_prefetch=1, grid=(d // tn, p // tm),
        in_specs=[pl.BlockSpec((tm, f), lambda j, i, m: (i, 0)),
                  pl.BlockSpec((1, f, tn), lambda j, i, m: (m[i], 0, j))],
        out_specs=pl.BlockSpec((tm, tn), lambda j, i, m: (i, j)))
    return pl.pallas_call(
        _moe_down_kernel, grid_spec=grid_spec,
        out_shape=jax.ShapeDtypeStruct((p, d), jnp.float32),
        compiler_params=_cparams("parallel", "parallel"), name="moe_down")(meta, hid, wd)


def _combine_kernel(pa_ref, pb_ref, pa_nxt, pb_nxt, y_hbm, h_ref, route_ref, o_ref, bufa, bufb, sem):
    i = pl.program_id(0)
    tm = bufa.shape[1]
    slot = lax.rem(i, 2)

    def issue(pa, pb, s):
        def body(r, c):
            _row_copy(y_hbm, pa[0, 0, r], bufa, s, r, sem.at[0, s]).start()
            _row_copy(y_hbm, pb[0, 0, r], bufb, s, r, sem.at[1, s]).start()
            return c
        lax.fori_loop(0, tm, body, 0)

    @pl.when(i == 0)
    def _():
        issue(pa_ref, pb_ref, 0)

    @pl.when(i + 1 < pl.num_programs(0))
    def _():
        issue(pa_nxt, pb_nxt, 1 - slot)

    pltpu.make_async_copy(y_hbm.at[pl.ds(0, tm)], bufa.at[slot], sem.at[0, slot]).wait()
    pltpu.make_async_copy(y_hbm.at[pl.ds(0, tm)], bufb.at[slot], sem.at[1, slot]).wait()
    route = route_ref[...]
    ga = route[:, ROUTE_LANE:ROUTE_LANE + 1]
    gb = route[:, ROUTE_LANE + 1:ROUTE_LANE + 2]
    o_ref[...] = h_ref[...] + (ga * bufa[slot] + gb * bufb[slot])


def _moe_combine(y, pos, h, route, tm):
    n, d = h.shape
    nt = n // tm
    pa = pos[:, 0].reshape(nt, 1, tm)
    pb = pos[:, 1].reshape(nt, 1, tm)
    cur = pl.BlockSpec((1, 1, tm), lambda i: (i, 0, 0), memory_space=pltpu.SMEM)
    nxt = pl.BlockSpec((1, 1, tm), lambda i: (jnp.minimum(i + 1, nt - 1), 0, 0), memory_space=pltpu.SMEM)
    return pl.pallas_call(
        _combine_kernel, grid=(nt,),
        in_specs=[cur, cur, nxt, nxt, pl.BlockSpec(memory_space=pl.ANY),
                  pl.BlockSpec((tm, d), lambda i: (i, 0)),
                  pl.BlockSpec((tm, LANES), lambda i: (i, 0))],
        out_specs=pl.BlockSpec((tm, d), lambda i: (i, 0)),
        out_shape=jax.ShapeDtypeStruct((n, d), jnp.float32),
        scratch_shapes=[pltpu.VMEM((2, tm, d), jnp.float32), pltpu.VMEM((2, tm, d), jnp.float32),
                        pltpu.SemaphoreType.DMA((2, 2))],
        compiler_params=_cparams("arbitrary"), name="moe_combine")(pa, pb, pa, pb, y, h, route)


def _moe(h, norm_gain, router_w, wg, wu, wd):
    hn, route = _rmsnorm_router(h, norm_gain, router_w)
    src, pos, meta = _route_tables(route, MOE_TM)
    xs = _moe_gather(hn, src)
    hid = _moe_gate_up(xs, wg.astype(MXU_DTYPE), wu.astype(MXU_DTYPE), meta, MOE_TM)
    y = _moe_down(hid, wd.astype(MXU_DTYPE), meta, MOE_TM)
    return _moe_combine(y, pos, h, route, COMBINE_TM)


def _sortable(x):
    x = jnp.where(x == 0.0, 0.0, x)
    i = lax.bitcast_convert_type(x, jnp.int32)
    return i ^ ((i >> 31) & jnp.int32(0x7FFFFFFF))


def _dsa_kernel(q_ref, kv_ref, iq_ref, ikk_ref, ikq_ref, nb_ref, o_ref,
                iklo_ref, ikhi_ref, vt_ref, wt_ref, skey_ref, hi_ref, lo_ref, m_ref, l_ref, a_ref,
                acc_ref, s0_ref, s1_ref, p_ref, *, n_top):
    tq = DSA_TQ
    half_keys = tq // 2
    qi = pl.program_id(1)
    f32 = jnp.float32
    nkb = skey_ref.shape[0]

    @pl.when(qi == 0)
    def _():
        ik = ikk_ref[...]
        lane = lax.broadcasted_iota(jnp.int32, ik.shape, 1)
        iklo_ref[...] = jnp.where(lane < IDX_DIM, ik, 0.0).astype(MXU_DTYPE)
        shifted = pltpu.roll(ik, IDX_DIM, axis=1)
        ikhi_ref[...] = jnp.where(lane >= IDX_DIM, shifted, 0.0).astype(MXU_DTYPE)
        for kb in range(nkb):
            for g in range(A_KV_HEADS):
                v = kv_ref[kb * tq:(kb + 1) * tq, A_KV + g * A_HEAD_DIM:A_KV + (g + 1) * A_HEAD_DIM]
                vt_ref[g, kb] = v.astype(f32).T.astype(MXU_DTYPE)

    wt_ref[...] = ikq_ref[...].T

    key_iota = lax.broadcasted_iota(jnp.int32, (half_keys, tq), 0)
    t_chunk = (qi * tq + lax.broadcasted_iota(jnp.int32, (half_keys, tq), 1)) // CHUNK

    def score_block(kb, carry):
        for half in range(2):
            r = pl.multiple_of(kb * tq + half * half_keys, half_keys)
            klo = iklo_ref[pl.ds(r, half_keys), :]
            khi = ikhi_ref[pl.ds(r, half_keys), :]
            acc = jnp.zeros((half_keys, tq), f32)
            for p in range(IDX_HEADS // 2):
                rhs = iq_ref[:, p * LANES:(p + 1) * LANES]
                for lhs, h in ((klo, 2 * p), (khi, 2 * p + 1)):
                    hs = lax.dot_general(lhs, rhs, _NT, preferred_element_type=f32)
                    acc = acc + jnp.maximum(hs, 0.0) * wt_ref[IDX_DIM + h:IDX_DIM + h + 1, :]
            key = _sortable(acc)
            s_chunk = (r + key_iota) // CHUNK
            rows = slice(half * half_keys, (half + 1) * half_keys)
            key = jnp.where(s_chunk <= t_chunk, key, INT_MIN)
            skey_ref[kb, rows, :] = key
            hi_ref[kb, rows, :] = (key >> 16).astype(jnp.int16)
        return carry

    lax.fori_loop(0, qi + 1, score_block, 0)

    n_groups = tq // SUBLANES
    n_acc = 4
    sub_iota = lax.broadcasted_iota(jnp.int32, (SUBLANES, tq), 0)

    def rows8(x):
        return jnp.broadcast_to(x, (SUBLANES, tq))

    def count_where(pred):
        def body(kb, accs):
            accs = list(accs)
            for g in range(n_groups):
                keys = skey_ref[kb, g * SUBLANES:(g + 1) * SUBLANES, :]
                hit = jnp.where(pred(keys, kb * tq + g * SUBLANES), 1.0, 0.0)
                accs[g % n_acc] = accs[g % n_acc] + hit
            return tuple(accs)

        accs = lax.fori_loop(0, qi + 1, body,
                             tuple(jnp.zeros((SUBLANES, tq), f32) for _ in range(n_acc)))
        return jnp.sum((accs[0] + accs[1]) + (accs[2] + accs[3]), axis=0, keepdims=True)

    i16 = jnp.int16
    n_groups16 = tq // PACKED_ROWS

    def rows16(x):
        return jnp.broadcast_to(x, (PACKED_ROWS, tq))

    def count16(ref, pred):
        def body(kb, accs):
            accs = list(accs)
            for g in range(n_groups16):
                rows = ref[kb, g * PACKED_ROWS:(g + 1) * PACKED_ROWS, :]
                hit = jnp.where(pred(rows), i16(1), i16(0))
                accs[g % n_acc] = accs[g % n_acc] + hit
            return tuple(accs)

        accs = lax.fori_loop(0, qi + 1, body,
                             tuple(jnp.zeros((PACKED_ROWS, tq), i16) for _ in range(n_acc)))
        tot = (accs[0] + accs[1]) + (accs[2] + accs[3])
        return jnp.sum(tot.astype(jnp.int32), axis=0, keepdims=True)

    def search16(ref, target):
        def step(it, tu):
            bit = jnp.left_shift(jnp.int32(1), 15 - it)
            cand = rows16(((tu | bit) - HALF16).astype(i16))
            cnt = count16(ref, lambda rows: rows >= cand)
            return jnp.where(cnt >= target, tu | bit, tu)

        return lax.fori_loop(0, 16, step, jnp.zeros((1, tq), jnp.int32))

    t_hi = search16(hi_ref, n_top) - HALF16
    t_hi16 = rows16(t_hi.astype(i16))
    above = count16(hi_ref, lambda rows: rows > t_hi16)
    t_hi8 = rows8(t_hi)

    def low_block(kb, carry):
        for g in range(n_groups16):
            rows = slice(g * PACKED_ROWS, (g + 1) * PACKED_ROWS)
            keys = skey_ref[kb, rows, :]
            t2 = jnp.concatenate([t_hi8, t_hi8], axis=0)
            low = jnp.where((keys >> 16) == t2, (keys & 0xFFFF) - HALF16, -HALF16)
            lo_ref[kb, rows, :] = low.astype(i16)
        return carry

    lax.fori_loop(0, qi + 1, low_block, 0)
    t_lo = search16(lo_ref, n_top - above)
    thr = rows8(t_hi * (2 * HALF16) + t_lo)

    n_gt = count_where(lambda keys, k0: keys > thr)
    n_eq = count_where(lambda keys, k0: keys == thr)
    need = n_top - n_gt
    n_bits = max(1, int(math.ceil(math.log2(nkb * tq + 1))))

    def tie_search():
        def jstep(it, jb):
            bit = jnp.left_shift(jnp.int32(1), n_bits - 1 - it)
            bound = rows8(jb | bit)
            cnt = count_where(lambda keys, k0: (keys == thr) & ((k0 + sub_iota) < bound))
            return jnp.where(cnt <= need, jb | bit, jb)

        return lax.fori_loop(0, n_bits, jstep, jnp.zeros((1, tq), jnp.int32))

    jb = lax.cond(jnp.max(n_eq - need) > 0.0, tie_search,
                  lambda: jnp.full((1, tq), 2 ** 30, jnp.int32))
    bound = rows8(jb)

    def mask_block(kb, carry):
        for g in range(n_groups):
            rows = slice(g * SUBLANES, (g + 1) * SUBLANES)
            keys = skey_ref[kb, rows, :]
            tie = (keys == thr) & ((kb * tq + g * SUBLANES + sub_iota) < bound)
            sel = ((keys > thr) | tie) & (keys != INT_MIN)
            skey_ref[kb, rows, :] = lax.bitcast_convert_type(
                jnp.where(sel, 0.0, MASKED).astype(f32), jnp.int32)
        return carry

    lax.fori_loop(0, qi + 1, mask_block, 0)

    m_ref[...] = jnp.full(m_ref.shape, MASKED, f32)
    l_ref[...] = jnp.zeros_like(l_ref)
    acc_ref[...] = jnp.zeros_like(acc_ref)
    grp = A_HEADS // A_KV_HEADS

    def logits(kb_raw, s_ref):
        kb = jnp.minimum(kb_raw, qi)
        r = pl.multiple_of(kb * tq, tq)
        nb_row = pl.multiple_of(jnp.clip(kb - (qi - 2), 0, 2) * tq, tq)
        beyond = jnp.where(kb_raw > qi, MASKED, 0.0).astype(f32)
        mb = lax.bitcast_convert_type(skey_ref[kb], f32) + beyond
        for h in range(A_HEADS):
            g = h // grp
            kblk = kv_ref[pl.ds(r, tq), g * A_HEAD_DIM:(g + 1) * A_HEAD_DIM]
            qh = q_ref[:, h * A_HEAD_DIM:(h + 1) * A_HEAD_DIM]
            s = lax.dot_general(kblk, qh, _NT, preferred_element_type=f32)
            s_ref[h] = s + (mb + nb_ref[h, pl.ds(nb_row, tq), :])

    def softmax_pv(kb_raw, s_ref):
        kb = jnp.minimum(kb_raw, qi)
        for h in range(A_HEADS):
            m_prev = m_ref[h]
            m_new = jnp.maximum(m_prev, jnp.max(s_ref[h], axis=0, keepdims=True))
            alpha = jnp.exp2(m_prev - m_new)
            p = jnp.exp2(s_ref[h] - m_new)
            l_ref[h] = alpha * l_ref[h] + jnp.sum(p, axis=0, keepdims=True)
            m_ref[h] = m_new
            a_ref[h] = alpha
            p_ref[h] = p.astype(MXU_DTYPE)
        for h in range(A_HEADS):
            vt = vt_ref[h // grp, kb]
            acc_ref[h] = a_ref[h] * acc_ref[h] + _dot(vt, p_ref[h])

    def attend_pair(j, carry):
        logits(2 * j + 1, s1_ref)
        softmax_pv(2 * j, s0_ref)
        logits(2 * j + 2, s0_ref)
        softmax_pv(2 * j + 1, s1_ref)
        return carry

    logits(0, s0_ref)
    lax.fori_loop(0, (qi + 2) // 2, attend_pair, 0)

    for h in range(A_HEADS):
        o = (acc_ref[h] / l_ref[h]).T
        o_ref[:, h * A_HEAD_DIM:(h + 1) * A_HEAD_DIM] = o.astype(o_ref.dtype)


def _t5_bucket(rel):
    half = REL_BUCKETS // 2
    max_exact = half // 2
    ret = jnp.where(rel > 0, half, 0)
    n = jnp.abs(rel)
    nf = jnp.maximum(n, 1).astype(jnp.float32)
    large = max_exact + (jnp.log(nf / max_exact) / math.log(REL_MAX_DIST / max_exact)
                         * (half - max_exact)).astype(jnp.int32)
    large = jnp.minimum(large, half - 1)
    return ret + jnp.where(n < max_exact, n, large)


def _near_bias(rel_bias):
    tq = DSA_TQ
    c = jnp.arange(2 * tq, dtype=jnp.int32)[:, None]
    i = jnp.arange(tq, dtype=jnp.int32)[None, :]
    rel = c - tq - i
    far = rel_bias[_t5_bucket(jnp.full((1,), -REL_MAX_DIST, jnp.int32))]
    onehot = (_t5_bucket(rel)[None] == jnp.arange(REL_BUCKETS, dtype=jnp.int32)[:, None, None])
    table = (rel_bias - far).T * LOG2E
    near = jnp.einsum('hb,bci->hci', table, onehot.astype(jnp.float32),
                      precision=lax.Precision.HIGHEST)
    return jnp.pad(near, ((0, 0), (tq, 0), (0, 0)))


def _dsa(proj, ikw, near_bias, batch, seq):
    tq = DSA_TQ
    nq = seq // tq
    n_top = min(TOPK_MAX, seq // 4)
    return pl.pallas_call(
        functools.partial(_dsa_kernel, n_top=n_top),
        grid=(batch, nq),
        in_specs=[pl.BlockSpec((tq, A_Q), lambda b, i: (b * nq + i, COL_Q // A_Q)),
                  pl.BlockSpec((seq, 2 * A_KV), lambda b, i: (b, COL_KV // (2 * A_KV))),
                  pl.BlockSpec((tq, IDX_Q), lambda b, i: (b * nq + i, COL_IQ // IDX_Q)),
                  pl.BlockSpec((seq, LANES), lambda b, i: (b, 0)),
                  pl.BlockSpec((tq, LANES), lambda b, i: (b * nq + i, 0)),
                  pl.BlockSpec((A_HEADS, 3 * tq, tq), lambda b, i: (0, 0, 0),
                               pipeline_mode=pl.Buffered(1))],
        out_specs=pl.BlockSpec((tq, A_Q), lambda b, i: (b * nq + i, 0)),
        out_shape=jax.ShapeDtypeStruct((batch * seq, A_Q), MXU_DTYPE),
        scratch_shapes=[pltpu.VMEM((seq, LANES), MXU_DTYPE),
                        pltpu.VMEM((seq, LANES), MXU_DTYPE),
                        pltpu.VMEM((A_KV_HEADS, nq, A_HEAD_DIM, tq), MXU_DTYPE),
                        pltpu.VMEM((LANES, tq), jnp.float32),
                        pltpu.VMEM((nq, tq, tq), jnp.int32),
                        pltpu.VMEM((nq, tq, tq), jnp.int16),
                        pltpu.VMEM((nq, tq, tq), jnp.int16),
                        pltpu.VMEM((A_HEADS, 1, tq), jnp.float32),
                        pltpu.VMEM((A_HEADS, 1, tq), jnp.float32),
                        pltpu.VMEM((A_HEADS, 1, tq), jnp.float32),
                        pltpu.VMEM((A_HEADS, A_HEAD_DIM, tq), jnp.float32),
                        pltpu.VMEM((A_HEADS, tq, tq), jnp.float32),
                        pltpu.VMEM((A_HEADS, tq, tq), jnp.float32),
                        pltpu.VMEM((A_HEADS, tq, tq), MXU_DTYPE)],
        compiler_params=_cparams("arbitrary", "arbitrary"), name="dsa")(
            proj, proj, proj, ikw, ikw, near_bias)


def _ret_kernel(qk_ref, vg_ref, cos_ref, sin_ref, gn_ref, o_ref, state_ref, dm_ref):
    c_len = RET_CHUNK
    f32 = jnp.float32
    hd = R_HEAD_DIM
    half = hd // 2
    log_g = [math.log(1.0 - 2.0 ** (-5.0 - h)) for h in range(R_HEADS)]

    @pl.when(pl.program_id(1) == 0)
    def _():
        state_ref[...] = jnp.zeros_like(state_ref)
        i = lax.broadcasted_iota(jnp.int32, (c_len, c_len), 0)
        j = lax.broadcasted_iota(jnp.int32, (c_len, c_len), 1)
        diff = (i - j).astype(f32)
        for h in range(R_HEADS):
            dm_ref[h] = jnp.where(diff >= 0, jnp.exp(log_g[h] * jnp.maximum(diff, 0.0)), 0.0)

    row = lax.broadcasted_iota(jnp.int32, (c_len, hd), 0).astype(f32)
    cos = cos_ref[...]
    sin = sin_ref[...]

    def rot(x):
        x1, x2 = x[:, :half], x[:, half:]
        return jnp.concatenate([x1 * cos - x2 * sin, x1 * sin + x2 * cos], axis=1)

    for h in range(R_HEADS):
        cols = slice(h * hd, (h + 1) * hd)
        qf = rot(qk_ref[:, cols].astype(f32))
        kf = rot(qk_ref[:, R_W + h * hd:R_W + (h + 1) * hd].astype(f32)) * (hd ** -0.5)
        v = vg_ref[:, cols]
        att = lax.dot_general(qf.astype(MXU_DTYPE), kf.astype(MXU_DTYPE), _NT,
                              preferred_element_type=f32) * dm_ref[h]
        xi = jnp.exp(log_g[h] * (row + 1.0))
        zeta = jnp.exp(log_g[h] * (c_len - 1.0 - row))
        state = state_ref[h]
        o = _dot(att.astype(MXU_DTYPE), v) + _dot((qf * xi).astype(MXU_DTYPE), state.astype(MXU_DTYPE))
        state_ref[h] = state * math.exp(log_g[h] * c_len) + lax.dot_general(
            (kf * zeta).astype(MXU_DTYPE), v, _TN, preferred_element_type=f32)
        mu = jnp.mean(o, axis=-1, keepdims=True)
        d = o - mu
        var = jnp.mean(d * d, axis=-1, keepdims=True)
        y = d * lax.rsqrt(var + GN_EPS) * gn_ref[:, cols]
        g = vg_ref[:, R_W + h * hd:R_W + (h + 1) * hd].astype(f32)
        o_ref[:, cols] = (g * jax.nn.sigmoid(g) * y).astype(o_ref.dtype)


def _rope_tables(seq):
    half = R_HEAD_DIM // 2
    inv = ROPE_BASE ** (-jnp.arange(half, dtype=jnp.float32) / half)
    ang = jnp.arange(seq, dtype=jnp.float32)[:, None] * inv[None, :]
    return jnp.cos(ang), jnp.sin(ang)


def _retention(proj, cos, sin, gn_gain, batch, seq):
    c_len = RET_CHUNK
    nc = seq // c_len
    half = R_HEAD_DIM // 2
    return pl.pallas_call(
        _ret_kernel, grid=(batch, nc),
        in_specs=[pl.BlockSpec((c_len, 2 * R_W), lambda b, c: (b * nc + c, COL_RQK // (2 * R_W))),
                  pl.BlockSpec((c_len, 2 * R_W), lambda b, c: (b * nc + c, COL_RVG // (2 * R_W))),
                  pl.BlockSpec((c_len, half), lambda b, c: (c, 0)),
                  pl.BlockSpec((c_len, half), lambda b, c: (c, 0)),
                  pl.BlockSpec((1, R_W), lambda b, c: (0, 0))],
        out_specs=pl.BlockSpec((c_len, R_W), lambda b, c: (b * nc + c, 0)),
        out_shape=jax.ShapeDtypeStruct((batch * seq, R_W), MXU_DTYPE),
        scratch_shapes=[pltpu.VMEM((R_HEADS, R_HEAD_DIM, R_HEAD_DIM), jnp.float32),
                        pltpu.VMEM((R_HEADS, c_len, c_len), jnp.float32)],
        compiler_params=_cparams("arbitrary", "arbitrary"), name="retention")(
            proj, proj, cos, sin, gn_gain.reshape(1, R_W))


def _mixer(h, norm_gain, w_in, q_gain, k_gain, gn_gain, w_out, near_bias, cos, sin, batch, seq):
    head_gains = jnp.stack([q_gain * (A_HEAD_DIM ** -0.5 * LOG2E), k_gain])
    proj, ikw = _project(h, norm_gain, _merged_in_weights(w_in), head_gains)
    a_out = _dsa(proj, ikw, near_bias, batch, seq)
    r_out = _retention(proj, cos, sin, gn_gain, batch, seq)
    return _out_project(a_out, r_out, w_out.astype(MXU_DTYPE), h)


def kernel(x, rel_bias, norm_mix, w_in, q_gain, k_gain, ret_gain, w_out, norm_ffn,
           ffn_w_gate, ffn_w_up, ffn_w_down, moe_router, moe_w_gate, moe_w_up, moe_w_down):
    batch, seq, d = x.shape
    depth = norm_mix.shape[0]
    h = x.reshape(batch * seq, d)
    near_bias = _near_bias(rel_bias)
    cos, sin = _rope_tables(seq)
    for l in range(depth):
        h = _mixer(h, norm_mix[l], w_in[l], q_gain[l], k_gain[l], ret_gain[l], w_out[l],
                   near_bias, cos, sin, batch, seq)
        j = l // 2
        if l % 2 == 0:
            hid = _gate_up(h, norm_ffn[l], ffn_w_gate[j].astype(MXU_DTYPE),
                           ffn_w_up[j].astype(MXU_DTYPE))
            f = ffn_w_down.shape[1]
            h = _down(hid, ffn_w_down[j].astype(MXU_DTYPE), h, tk=f // 2)
        else:
            h = _moe(h, norm_ffn[l], moe_router[j], moe_w_gate[j], moe_w_up[j], moe_w_down[j])
    return h.reshape(batch, seq, d)
```

```python
import functools
import math

import jax
import jax.numpy as jnp
import numpy as np
from jax import lax
from jax.experimental import pallas as pl
from jax.experimental.pallas import tpu as pltpu

CHUNK = 64
A_HEADS = 8
A_KV_HEADS = 2
A_HEAD_DIM = 128
IDX_HEADS = 16
IDX_DIM = 64
TOPK_MAX = 256
REL_BUCKETS = 32
REL_MAX_DIST = 128
R_HEADS = 4
R_HEAD_DIM = 256
ROPE_BASE = 10000.0
A_Q = A_HEADS * A_HEAD_DIM
A_KV = A_KV_HEADS * A_HEAD_DIM
IDX_Q = IDX_HEADS * IDX_DIM
R_W = R_HEADS * R_HEAD_DIM
N_EXPERTS = 8
EPS = 1e-6
GN_EPS = 1e-5

LANES = 128
SUBLANES = 8
PACKED_ROWS = 16
HALF16 = 1 << 15
LOG2E = math.log2(math.e)
VMEM_LIMIT_BYTES = 56 * 1024 * 1024
MOE_GU_VMEM_LIMIT_BYTES = 60 * 1024 * 1024
MXU_DTYPE = jnp.bfloat16
DSA_TQ = 256
RET_CHUNK = 256
MOE_TM = 512
COMBINE_TM = 256
ROUTE_LANE = 8
INT_MIN = -2147483648
MASKED = -1e30

_NT = (((1,), (1,)), ((), ()))
_TN = (((0,), (0,)), ((), ()))


def _cparams(*sem):
    return pltpu.CompilerParams(dimension_semantics=sem, vmem_limit_bytes=VMEM_LIMIT_BYTES)


def _dot(a, b):
    return jnp.dot(a, b, preferred_element_type=jnp.float32)


def _rms(x_ref, g_ref):
    x = x_ref[...]
    ms = jnp.mean(x * x, axis=-1, keepdims=True)
    return x * lax.rsqrt(ms + EPS) * g_ref[...]


def _norm_router_kernel(x_ref, g_ref, wr_ref, o_ref, gates_ref):
    xn = _rms(x_ref, g_ref)
    o_ref[...] = xn.astype(o_ref.dtype)
    x_hi = xn.astype(jnp.bfloat16)
    x_lo = (xn - x_hi.astype(jnp.float32)).astype(jnp.bfloat16)
    w = wr_ref[...]
    w_hi = w.astype(jnp.bfloat16)
    w_lo = (w - w_hi.astype(jnp.float32)).astype(jnp.bfloat16)
    logits = _dot(x_hi, w_hi) + (_dot(x_lo, w_hi) + _dot(x_hi, w_lo))
    lane = lax.broadcasted_iota(jnp.int32, logits.shape, 1)
    neg = jnp.float32(-jnp.inf)
    logits = jnp.where(lane < N_EXPERTS, logits, neg)
    m1 = jnp.max(logits, axis=-1, keepdims=True)
    i1 = jnp.min(jnp.where(logits == m1, lane, LANES), axis=-1, keepdims=True)
    rest = jnp.where(lane == i1, neg, logits)
    m2 = jnp.max(rest, axis=-1, keepdims=True)
    i2 = jnp.min(jnp.where(rest == m2, lane, LANES), axis=-1, keepdims=True)
    e2 = jnp.exp(m2 - m1)
    g1 = 1.0 / (1.0 + e2)
    g2 = e2 / (1.0 + e2)
    route = jnp.where(lane == ROUTE_LANE, g1, 0.0) + jnp.where(lane == ROUTE_LANE + 1, g2, 0.0)
    route = route + jnp.where(lane == ROUTE_LANE + 2, i1.astype(jnp.float32), 0.0)
    route = route + jnp.where(lane == ROUTE_LANE + 3, i2.astype(jnp.float32), 0.0)
    gates_ref[...] = route


def _rmsnorm_router(x, gain, router_w, tm=512):
    n, d = x.shape
    wr = jnp.pad(router_w, ((0, 0), (0, LANES - router_w.shape[1])))
    return pl.pallas_call(
        _norm_router_kernel, grid=(n // tm,),
        in_specs=[pl.BlockSpec((tm, d), lambda i: (i, 0)),
                  pl.BlockSpec((1, d), lambda i: (0, 0)),
                  pl.BlockSpec((d, LANES), lambda i: (0, 0))],
        out_specs=[pl.BlockSpec((tm, d), lambda i: (i, 0)),
                   pl.BlockSpec((tm, LANES), lambda i: (i, 0))],
        out_shape=[jax.ShapeDtypeStruct((n, d), jnp.float32),
                   jax.ShapeDtypeStruct((n, LANES), jnp.float32)],
        compiler_params=_cparams("parallel"), name="rmsnorm_router")(x, gain.reshape(1, d), wr)


PROJ_TN = 512
COL_RQK = 0
COL_RVG = COL_RQK + 2 * R_W
COL_Q = COL_RVG + 2 * R_W
COL_IQ = COL_Q + A_Q
COL_KV = COL_IQ + IDX_Q
COL_IKW = COL_KV + 2 * A_KV
PROJ_WIDTH = COL_IKW + PROJ_TN


def _proj_kernel(x_ref, g_ref, w_ref, hg_ref, o_ref, ikw_ref, xn_ref):
    j = pl.program_id(1)

    @pl.when(j == 0)
    def _():
        xn_ref[...] = _rms(x_ref, g_ref).astype(xn_ref.dtype)

    acc = _dot(xn_ref[...], w_ref[...])
    o_ref[...] = acc.astype(o_ref.dtype)

    def renorm(n_heads, gain_row):
        for h in range(n_heads):
            y = acc[:, h * A_HEAD_DIM:(h + 1) * A_HEAD_DIM]
            ms = jnp.mean(y * y, axis=-1, keepdims=True)
            y = y * lax.rsqrt(ms + EPS) * hg_ref[gain_row:gain_row + 1, :]
            o_ref[:, h * A_HEAD_DIM:(h + 1) * A_HEAD_DIM] = y.astype(o_ref.dtype)

    @pl.when(jnp.logical_and(j >= COL_Q // PROJ_TN, j < COL_IQ // PROJ_TN))
    def _():
        renorm(PROJ_TN // A_HEAD_DIM, 0)

    @pl.when(j == COL_KV // PROJ_TN)
    def _():
        renorm(A_KV // A_HEAD_DIM, 1)

    @pl.when(j == COL_IKW // PROJ_TN)
    def _():
        ikw_ref[...] = acc[:, :LANES]


def _project(x, norm_gain, w_all, head_gains, tm=1024):
    n, d = x.shape
    return pl.pallas_call(
        _proj_kernel, grid=(n // tm, PROJ_WIDTH // PROJ_TN),
        in_specs=[pl.BlockSpec((tm, d), lambda i, j: (i, 0)),
                  pl.BlockSpec((1, d), lambda i, j: (0, 0)),
                  pl.BlockSpec((d, PROJ_TN), lambda i, j: (0, j)),
                  pl.BlockSpec((2, A_HEAD_DIM), lambda i, j: (0, 0))],
        out_specs=[pl.BlockSpec((tm, PROJ_TN), lambda i, j: (i, j)),
                   pl.BlockSpec((tm, LANES), lambda i, j: (i, 0))],
        out_shape=[jax.ShapeDtypeStruct((n, PROJ_WIDTH), MXU_DTYPE),
                   jax.ShapeDtypeStruct((n, LANES), jnp.float32)],
        scratch_shapes=[pltpu.VMEM((tm, d), MXU_DTYPE)],
        compiler_params=_cparams("parallel", "arbitrary"), name="in_proj")(
            x, norm_gain.reshape(1, d), w_all, head_gains)


def _merged_in_weights(w_in):
    c0 = A_Q
    c1 = c0 + 2 * A_KV
    c2 = c1 + IDX_Q
    c3 = c2 + IDX_DIM + IDX_HEADS
    c4 = c3 + 2 * R_W
    ikw = jnp.pad(w_in[:, c2:c3], ((0, 0), (0, PROJ_TN - (c3 - c2))))
    parts = [w_in[:, c3:c4], w_in[:, c4:], w_in[:, :c0], w_in[:, c1:c2], w_in[:, c0:c1], ikw]
    return jnp.concatenate(parts, axis=1).astype(MXU_DTYPE)


def _outproj_kernel(a_ref, r_ref, wa_ref, wr_ref, res_ref, o_ref):
    o_ref[...] = res_ref[...] + (_dot(a_ref[...], wa_ref[...]) + _dot(r_ref[...], wr_ref[...]))


def _out_project(a, r, w, res, tm=512, tn=2048):
    n, ka = a.shape
    kr = r.shape[1]
    assert ka == kr
    d = w.shape[1]
    tn = min(tn, d)
    return pl.pallas_call(
        _outproj_kernel, grid=(n // tm, d // tn),
        in_specs=[pl.BlockSpec((tm, ka), lambda i, j: (i, 0)),
                  pl.BlockSpec((tm, kr), lambda i, j: (i, 0)),
                  pl.BlockSpec((ka, tn), lambda i, j: (0, j)),
                  pl.BlockSpec((kr, tn), lambda i, j: (1, j)),
                  pl.BlockSpec((tm, tn), lambda i, j: (i, j))],
        out_specs=pl.BlockSpec((tm, tn), lambda i, j: (i, j)),
        out_shape=jax.ShapeDtypeStruct((n, d), jnp.float32),
        compiler_params=_cparams("parallel", "parallel"), name="out_proj")(a, r, w, w, res)


def _swiglu_hidden(x, wg, wu):
    g = _dot(x, wg)
    u = _dot(x, wu)
    return g * jax.nn.sigmoid(g) * u


def _gu_kernel(x_ref, g_ref, wg_ref, wu_ref, o_ref, xn_ref):
    @pl.when(pl.program_id(1) == 0)
    def _():
        xn_ref[...] = _rms(x_ref, g_ref).astype(xn_ref.dtype)

    o_ref[...] = _swiglu_hidden(xn_ref[...], wg_ref[...], wu_ref[...]).astype(o_ref.dtype)


def _gate_up(x, norm_gain, wg, wu, tm=1024, tn=512):
    n, d = x.shape
    f = wg.shape[1]
    return pl.pallas_call(
        _gu_kernel, grid=(n // tm, f // tn),
        in_specs=[pl.BlockSpec((tm, d), lambda i, j: (i, 0)),
                  pl.BlockSpec((1, d), lambda i, j: (0, 0)),
                  pl.BlockSpec((d, tn), lambda i, j: (0, j)),
                  pl.BlockSpec((d, tn), lambda i, j: (0, j))],
        out_specs=pl.BlockSpec((tm, tn), lambda i, j: (i, j)),
        out_shape=jax.ShapeDtypeStruct((n, f), MXU_DTYPE),
        scratch_shapes=[pltpu.VMEM((tm, d), MXU_DTYPE)],
        compiler_params=_cparams("parallel", "arbitrary"), name="gate_up")(
            x, norm_gain.reshape(1, d), wg, wu)


def _down_kernel(x_ref, w_ref, res_ref, o_ref):
    o_ref[...] = res_ref[...] + _dot(x_ref[...], w_ref[...])


def _down(x, w, res, tm=512, tn=1024):
    n, kt = x.shape
    d = w.shape[1]
    tn = min(tn, d)
    return pl.pallas_call(
        _down_kernel, grid=(d // tn, n // tm),
        in_specs=[pl.BlockSpec((tm, kt), lambda j, i: (i, 0)),
                  pl.BlockSpec((kt, tn), lambda j, i: (0, j)),
                  pl.BlockSpec((tm, tn), lambda j, i: (i, j))],
        out_specs=pl.BlockSpec((tm, tn), lambda j, i: (i, j)),
        out_shape=jax.ShapeDtypeStruct((n, d), jnp.float32),
        compiler_params=_cparams("parallel", "parallel"), name="down_proj")(x, w, res)


def _route_tables(route, tm):
    n = route.shape[0]
    e = route[:, ROUTE_LANE + 2:ROUTE_LANE + 4].astype(jnp.int32).reshape(-1)
    onehot = (e[:, None] == jnp.arange(N_EXPERTS, dtype=jnp.int32)[None, :]).astype(jnp.int32)
    csum = jnp.cumsum(onehot, axis=0)
    rank = jnp.sum(csum * onehot, axis=1) - 1
    tiles_per = (csum[-1] + tm - 1) // tm
    tile_end = jnp.cumsum(tiles_per)
    row_start = (tile_end - tiles_per) * tm
    pos = jnp.sum(onehot * row_start[None, :], axis=1) + rank
    n_tiles = (2 * n) // tm + N_EXPERTS
    t = jnp.arange(n_tiles, dtype=jnp.int32)
    tile_expert = jnp.sum((t[:, None] >= tile_end[None, :]).astype(jnp.int32), axis=1)
    meta = jnp.concatenate([jnp.minimum(tile_expert, N_EXPERTS - 1), tile_end[-1:]]).astype(jnp.int32)
    src = jnp.zeros((n_tiles * tm,), jnp.int32).at[pos].set(
        jnp.arange(2 * n, dtype=jnp.int32) // 2, unique_indices=True)
    return src.reshape(n_tiles, 1, tm), pos.reshape(n, 2), meta


def _row_copy(src_hbm, row, dst, slot, r, sem):
    return pltpu.make_async_copy(src_hbm.at[pl.ds(row, 1)], dst.at[slot, pl.ds(r, 1)], sem)


def _gather_kernel(idx_ref, nxt_ref, x_hbm, o_ref, buf, sem):
    i = pl.program_id(0)
    tm = buf.shape[1]
    slot = lax.rem(i, 2)

    def issue(ref, s):
        def body(k, c):
            for u in range(2):
                r = 2 * k + u
                _row_copy(x_hbm, ref[0, 0, r], buf, s, r, sem.at[s]).start(priority=u)
            return c
        lax.fori_loop(0, tm // 2, body, 0, unroll=4)

    @pl.when(i == 0)
    def _():
        issue(idx_ref, 0)

    @pl.when(i + 1 < pl.num_programs(0))
    def _():
        issue(nxt_ref, 1 - slot)

    pltpu.make_async_copy(x_hbm.at[pl.ds(0, tm)], buf.at[slot], sem.at[slot]).wait()
    o_ref[...] = buf[slot].astype(o_ref.dtype)


def _moe_gather(x, src):
    n_tiles, _, tm = src.shape
    d = x.shape[1]
    idx_spec = pl.BlockSpec((1, 1, tm), lambda i: (i, 0, 0), memory_space=pltpu.SMEM)
    nxt_spec = pl.BlockSpec((1, 1, tm), lambda i: (jnp.minimum(i + 1, n_tiles - 1), 0, 0),
                            memory_space=pltpu.SMEM)
    return pl.pallas_call(
        _gather_kernel, grid=(n_tiles,),
        in_specs=[idx_spec, nxt_spec, pl.BlockSpec(memory_space=pl.ANY)],
        out_specs=pl.BlockSpec((tm, d), lambda i: (i, 0)),
        out_shape=jax.ShapeDtypeStruct((n_tiles * tm, d), MXU_DTYPE),
        scratch_shapes=[pltpu.VMEM((2, tm, d), x.dtype), pltpu.SemaphoreType.DMA((2,))],
        compiler_params=_cparams("arbitrary"), name="moe_gather")(src, src, x)


def _moe_gu_kernel(meta_ref, x_ref, wg_ref, wu_ref, o_ref, wgb_ref, wub_ref):
    i = pl.program_id(1)
    used = i < meta_ref[pl.num_programs(1)]
    new_block = jnp.logical_or(i == 0, meta_ref[i] != meta_ref[jnp.maximum(i - 1, 0)])

    @pl.when(jnp.logical_and(used, new_block))
    def _():
        wgb_ref[...] = wg_ref[0].astype(wgb_ref.dtype)
        wub_ref[...] = wu_ref[0].astype(wub_ref.dtype)

    @pl.when(used)
    def _():
        o_ref[...] = _swiglu_hidden(x_ref[...], wgb_ref[...], wub_ref[...]).astype(o_ref.dtype)

    @pl.when(jnp.logical_not(used))
    def _():
        o_ref[...] = jnp.zeros_like(o_ref)


def _moe_gate_up(xs, wg, wu, meta, tm, tn=1024):
    p, d = xs.shape
    f = wg.shape[2]
    tn = min(tn, f)
    grid_spec = pltpu.PrefetchScalarGridSpec(
        num_scalar_prefetch=1, grid=(f // tn, p // tm),
        in_specs=[pl.BlockSpec((tm, d), lambda j, i, m: (i, 0)),
                  pl.BlockSpec((1, d, tn), lambda j, i, m: (m[i], 0, j)),
                  pl.BlockSpec((1, d, tn), lambda j, i, m: (m[i], 0, j))],
        out_specs=pl.BlockSpec((tm, tn), lambda j, i, m: (i, j)),
        scratch_shapes=[pltpu.VMEM((d, tn), MXU_DTYPE), pltpu.VMEM((d, tn), MXU_DTYPE)])
    return pl.pallas_call(
        _moe_gu_kernel, grid_spec=grid_spec,
        out_shape=jax.ShapeDtypeStruct((p, f), MXU_DTYPE),
        compiler_params=pltpu.CompilerParams(dimension_semantics=("arbitrary", "arbitrary"),
                                             vmem_limit_bytes=MOE_GU_VMEM_LIMIT_BYTES),
        name="moe_gate_up")(meta, xs, wg, wu)


def _moe_down_kernel(meta_ref, x_ref, w_ref, o_ref):
    used = pl.program_id(1) < meta_ref[pl.num_programs(1)]

    @pl.when(used)
    def _():
        o_ref[...] = _dot(x_ref[...], w_ref[0])

    @pl.when(jnp.logical_not(used))
    def _():
        o_ref[...] = jnp.zeros_like(o_ref)


def _moe_down(hid, wd, meta, tm, tn=512):
    p, f = hid.shape
    d = wd.shape[2]
    tn = min(tn, d)
    grid_spec = pltpu.PrefetchScalarGridSpec(
        num_scalar_prefetch=1, grid=(d // tn, p // tm),
        in_specs=[pl.BlockSpec((tm, f), lambda j, i, m: (i, 0)),
                  pl.BlockSpec((1, f, tn), lambda j, i, m: (m[i], 0, j))],
        out_specs=pl.BlockSpec((tm, tn), lambda j, i, m: (i, j)))
    return pl.pallas_call(
        _moe_down_kernel, grid_spec=grid_spec,
        out_shape=jax.ShapeDtypeStruct((p, d), jnp.float32),
        compiler_params=_cparams("parallel", "parallel"), name="moe_down")(meta, hid, wd)


def _combine_kernel(pa_ref, pb_ref, pa_nxt, pb_nxt, y_hbm, h_ref, route_ref, o_ref, bufa, bufb, sem):
    i = pl.program_id(0)
    tm = bufa.shape[1]
    slot = lax.rem(i, 2)

    def issue(pa, pb, s):
        def body(r, c):
            _row_copy(y_hbm, pa[0, 0, r], bufa, s, r, sem.at[0, s]).start(priority=0)
            _row_copy(y_hbm, pb[0, 0, r], bufb, s, r, sem.at[1, s]).start(priority=1)
            return c
        lax.fori_loop(0, tm, body, 0, unroll=4)

    @pl.when(i == 0)
    def _():
        issue(pa_ref, pb_ref, 0)

    @pl.when(i + 1 < pl.num_programs(0))
    def _():
        issue(pa_nxt, pb_nxt, 1 - slot)

    pltpu.make_async_copy(y_hbm.at[pl.ds(0, tm)], bufa.at[slot], sem.at[0, slot]).wait()
    pltpu.make_async_copy(y_hbm.at[pl.ds(0, tm)], bufb.at[slot], sem.at[1, slot]).wait()
    route = route_ref[...]
    ga = route[:, ROUTE_LANE:ROUTE_LANE + 1]
    gb = route[:, ROUTE_LANE + 1:ROUTE_LANE + 2]
    o_ref[...] = h_ref[...] + (ga * bufa[slot] + gb * bufb[slot])


def _moe_combine(y, pos, h, route, tm):
    n, d = h.shape
    nt = n // tm
    pa = pos[:, 0].reshape(nt, 1, tm)
    pb = pos[:, 1].reshape(nt, 1, tm)
    cur = pl.BlockSpec((1, 1, tm), lambda i: (i, 0, 0), memory_space=pltpu.SMEM)
    nxt = pl.BlockSpec((1, 1, tm), lambda i: (jnp.minimum(i + 1, nt - 1), 0, 0), memory_space=pltpu.SMEM)
    return pl.pallas_call(
        _combine_kernel, grid=(nt,),
        in_specs=[cur, cur, nxt, nxt, pl.BlockSpec(memory_space=pl.ANY),
                  pl.BlockSpec((tm, d), lambda i: (i, 0)),
                  pl.BlockSpec((tm, LANES), lambda i: (i, 0))],
        out_specs=pl.BlockSpec((tm, d), lambda i: (i, 0)),
        out_shape=jax.ShapeDtypeStruct((n, d), jnp.float32),
        scratch_shapes=[pltpu.VMEM((2, tm, d), jnp.float32), pltpu.VMEM((2, tm, d), jnp.float32),
                        pltpu.SemaphoreType.DMA((2, 2))],
        compiler_params=_cparams("arbitrary"), name="moe_combine")(pa, pb, pa, pb, y, h, route)


def _moe(h, norm_gain, router_w, wg, wu, wd):
    hn, route = _rmsnorm_router(h, norm_gain, router_w)
    src, pos, meta = _route_tables(route, MOE_TM)
    xs = _moe_gather(hn, src)
    hid = _moe_gate_up(xs, wg, wu, meta, MOE_TM)
    y = _moe_down(hid, wd.astype(MXU_DTYPE), meta, MOE_TM)
    return _moe_combine(y, pos, h, route, COMBINE_TM)


def _sortable(x):
    x = jnp.where(x == 0.0, 0.0, x)
    i = lax.bitcast_convert_type(x, jnp.int32)
    return i ^ ((i >> 31) & jnp.int32(0x7FFFFFFF))


def _dsa_kernel(q_ref, kv_ref, iq_ref, ikk_ref, ikq_ref, nb_ref, o_ref,
                iklo_ref, ikhi_ref, vt_ref, wt_ref, skey_ref, hi_ref, lo_ref, m_ref, l_ref, a_ref,
                acc_ref, s0_ref, s1_ref, p_ref, *, n_top):
    tq = DSA_TQ
    half_keys = tq // 2
    qi = pl.program_id(1)
    f32 = jnp.float32
    nkb = skey_ref.shape[0]

    @pl.when(qi == 0)
    def _():
        ik = ikk_ref[...]
        lane = lax.broadcasted_iota(jnp.int32, ik.shape, 1)
        iklo_ref[...] = jnp.where(lane < IDX_DIM, ik, 0.0).astype(MXU_DTYPE)
        shifted = pltpu.roll(ik, IDX_DIM, axis=1)
        ikhi_ref[...] = jnp.where(lane >= IDX_DIM, shifted, 0.0).astype(MXU_DTYPE)
        for kb in range(nkb):
            for g in range(A_KV_HEADS):
                v = kv_ref[kb * tq:(kb + 1) * tq, A_KV + g * A_HEAD_DIM:A_KV + (g + 1) * A_HEAD_DIM]
                vt_ref[g, kb] = v.astype(f32).T.astype(MXU_DTYPE)

    wt_ref[...] = ikq_ref[...].T

    key_iota = lax.broadcasted_iota(jnp.int32, (half_keys, tq), 0)
    t_chunk = (qi * tq + lax.broadcasted_iota(jnp.int32, (half_keys, tq), 1)) // CHUNK

    def score_block(kb, carry):
        for half in range(2):
            r = pl.multiple_of(kb * tq + half * half_keys, half_keys)
            klo = iklo_ref[pl.ds(r, half_keys), :]
            khi = ikhi_ref[pl.ds(r, half_keys), :]
            acc = jnp.zeros((half_keys, tq), f32)
            for p in range(IDX_HEADS // 2):
                rhs = iq_ref[:, p * LANES:(p + 1) * LANES]
                for lhs, h in ((klo, 2 * p), (khi, 2 * p + 1)):
                    hs = lax.dot_general(lhs, rhs, _NT, preferred_element_type=f32)
                    acc = acc + jnp.maximum(hs, 0.0) * wt_ref[IDX_DIM + h:IDX_DIM + h + 1, :]
            key = _sortable(acc)
            s_chunk = (r + key_iota) // CHUNK
            rows = slice(half * half_keys, (half + 1) * half_keys)
            key = jnp.where(s_chunk <= t_chunk, key, INT_MIN)
            skey_ref[kb, rows, :] = key
            hi_ref[kb, rows, :] = (key >> 16).astype(jnp.int16)
        return carry

    lax.fori_loop(0, qi + 1, score_block, 0)

    n_groups = tq // SUBLANES
    n_acc = 4
    sub_iota = lax.broadcasted_iota(jnp.int32, (SUBLANES, tq), 0)

    def rows8(x):
        return jnp.broadcast_to(x, (SUBLANES, tq))

    def count_where(pred):
        def body(kb, accs):
            accs = list(accs)
            for g in range(n_groups):
                keys = skey_ref[kb, g * SUBLANES:(g + 1) * SUBLANES, :]
                hit = jnp.where(pred(keys, kb * tq + g * SUBLANES), 1.0, 0.0)
                accs[g % n_acc] = accs[g % n_acc] + hit
            return tuple(accs)

        accs = lax.fori_loop(0, qi + 1, body,
                             tuple(jnp.zeros((SUBLANES, tq), f32) for _ in range(n_acc)))
        return jnp.sum((accs[0] + accs[1]) + (accs[2] + accs[3]), axis=0, keepdims=True)

    i16 = jnp.int16
    n_groups16 = tq // PACKED_ROWS

    def rows16(x):
        return jnp.broadcast_to(x, (PACKED_ROWS, tq))

    def count16(ref, pred):
        def body(kb, accs):
            accs = list(accs)
            for g in range(n_groups16):
                rows = ref[kb, g * PACKED_ROWS:(g + 1) * PACKED_ROWS, :]
                hit = jnp.where(pred(rows), i16(1), i16(0))
                accs[g % n_acc] = accs[g % n_acc] + hit
            return tuple(accs)

        accs = lax.fori_loop(0, qi + 1, body,
                             tuple(jnp.zeros((PACKED_ROWS, tq), i16) for _ in range(n_acc)))
        tot = (accs[0] + accs[1]) + (accs[2] + accs[3])
        return jnp.sum(tot.astype(jnp.int32), axis=0, keepdims=True)

    def search16(ref, target):
        def step(it, tu):
            bit = jnp.left_shift(jnp.int32(1), 15 - it)
            cand = rows16(((tu | bit) - HALF16).astype(i16))
            cnt = count16(ref, lambda rows: rows >= cand)
            return jnp.where(cnt >= target, tu | bit, tu)

        return lax.fori_loop(0, 16, step, jnp.zeros((1, tq), jnp.int32))

    t_hi = search16(hi_ref, n_top) - HALF16
    t_hi16 = rows16(t_hi.astype(i16))
    above = count16(hi_ref, lambda rows: rows > t_hi16)
    t_hi8 = rows8(t_hi)

    def low_block(kb, carry):
        for g in range(n_groups16):
            rows = slice(g * PACKED_ROWS, (g + 1) * PACKED_ROWS)
            keys = skey_ref[kb, rows, :]
            t2 = jnp.concatenate([t_hi8, t_hi8], axis=0)
            low = jnp.where((keys >> 16) == t2, (keys & 0xFFFF) - HALF16, -HALF16)
            lo_ref[kb, rows, :] = low.astype(i16)
        return carry

    lax.fori_loop(0, qi + 1, low_block, 0)
    t_lo = search16(lo_ref, n_top - above)
    thr = rows8(t_hi * (2 * HALF16) + t_lo)

    n_gt = count_where(lambda keys, k0: keys > thr)
    n_eq = count_where(lambda keys, k0: keys == thr)
    need = n_top - n_gt
    n_bits = max(1, int(math.ceil(math.log2(nkb * tq + 1))))

    def tie_search():
        def jstep(it, jb):
            bit = jnp.left_shift(jnp.int32(1), n_bits - 1 - it)
            bound = rows8(jb | bit)
            cnt = count_where(lambda keys, k0: (keys == thr) & ((k0 + sub_iota) < bound))
            return jnp.where(cnt <= need, jb | bit, jb)

        return lax.fori_loop(0, n_bits, jstep, jnp.zeros((1, tq), jnp.int32))

    jb = lax.cond(jnp.max(n_eq - need) > 0.0, tie_search,
                  lambda: jnp.full((1, tq), 2 ** 30, jnp.int32))
    bound = rows8(jb)

    def mask_block(kb, carry):
        for g in range(n_groups):
            rows = slice(g * SUBLANES, (g + 1) * SUBLANES)
            keys = skey_ref[kb, rows, :]
            tie = (keys == thr) & ((kb * tq + g * SUBLANES + sub_iota) < bound)
            sel = ((keys > thr) | tie) & (keys != INT_MIN)
            skey_ref[kb, rows, :] = lax.bitcast_convert_type(
                jnp.where(sel, 0.0, MASKED).astype(f32), jnp.int32)
        return carry

    lax.fori_loop(0, qi + 1, mask_block, 0)

    m_ref[...] = jnp.full(m_ref.shape, MASKED, f32)
    l_ref[...] = jnp.zeros_like(l_ref)
    acc_ref[...] = jnp.zeros_like(acc_ref)
    grp = A_HEADS // A_KV_HEADS

    def logits(kb_raw, s_ref):
        kb = jnp.minimum(kb_raw, qi)
        r = pl.multiple_of(kb * tq, tq)
        nb_row = pl.multiple_of(jnp.clip(kb - (qi - 2), 0, 2) * tq, tq)
        beyond = jnp.where(kb_raw > qi, MASKED, 0.0).astype(f32)
        mb = lax.bitcast_convert_type(skey_ref[kb], f32) + beyond
        for h in range(A_HEADS):
            g = h // grp
            kblk = kv_ref[pl.ds(r, tq), g * A_HEAD_DIM:(g + 1) * A_HEAD_DIM]
            qh = q_ref[:, h * A_HEAD_DIM:(h + 1) * A_HEAD_DIM]
            s = lax.dot_general(kblk, qh, _NT, preferred_element_type=f32)
            s_ref[h] = s + (mb + nb_ref[h, pl.ds(nb_row, tq), :])

    def softmax_pv(kb_raw, s_ref):
        kb = jnp.minimum(kb_raw, qi)
        for h in range(A_HEADS):
            m_prev = m_ref[h]
            m_new = jnp.maximum(m_prev, jnp.max(s_ref[h], axis=0, keepdims=True))
            alpha = jnp.exp2(m_prev - m_new)
            p = jnp.exp2(s_ref[h] - m_new)
            l_ref[h] = alpha * l_ref[h] + jnp.sum(p, axis=0, keepdims=True)
            m_ref[h] = m_new
            a_ref[h] = alpha
            p_ref[h] = p.astype(MXU_DTYPE)
        for h in range(A_HEADS):
            vt = vt_ref[h // grp, kb]
            acc_ref[h] = a_ref[h] * acc_ref[h] + _dot(vt, p_ref[h])

    def attend_pair(j, carry):
        logits(2 * j + 1, s1_ref)
        softmax_pv(2 * j, s0_ref)
        logits(2 * j + 2, s0_ref)
        softmax_pv(2 * j + 1, s1_ref)
        return carry

    logits(0, s0_ref)
    lax.fori_loop(0, (qi + 2) // 2, attend_pair, 0)

    for h in range(A_HEADS):
        o = (acc_ref[h] / l_ref[h]).T
        o_ref[:, h * A_HEAD_DIM:(h + 1) * A_HEAD_DIM] = o.astype(o_ref.dtype)


def _t5_bucket(rel):
    half = REL_BUCKETS // 2
    max_exact = half // 2
    ret = jnp.where(rel > 0, half, 0)
    n = jnp.abs(rel)
    nf = jnp.maximum(n, 1).astype(jnp.float32)
    large = max_exact + (jnp.log(nf / max_exact) / math.log(REL_MAX_DIST / max_exact)
                         * (half - max_exact)).astype(jnp.int32)
    large = jnp.minimum(large, half - 1)
    return ret + jnp.where(n < max_exact, n, large)


def _near_bias(rel_bias):
    tq = DSA_TQ
    c = jnp.arange(2 * tq, dtype=jnp.int32)[:, None]
    i = jnp.arange(tq, dtype=jnp.int32)[None, :]
    rel = c - tq - i
    far = rel_bias[_t5_bucket(jnp.full((1,), -REL_MAX_DIST, jnp.int32))]
    onehot = (_t5_bucket(rel)[None] == jnp.arange(REL_BUCKETS, dtype=jnp.int32)[:, None, None])
    table = (rel_bias - far).T * LOG2E
    near = jnp.einsum('hb,bci->hci', table, onehot.astype(jnp.float32),
                      precision=lax.Precision.HIGHEST)
    return jnp.pad(near, ((0, 0), (tq, 0), (0, 0)))


def _dsa(proj, ikw, near_bias, batch, seq):
    tq = DSA_TQ
    nq = seq // tq
    n_top = min(TOPK_MAX, seq // 4)
    return pl.pallas_call(
        functools.partial(_dsa_kernel, n_top=n_top),
        grid=(batch, nq),
        in_specs=[pl.BlockSpec((tq, A_Q), lambda b, i: (b * nq + i, COL_Q // A_Q)),
                  pl.BlockSpec((seq, 2 * A_KV), lambda b, i: (b, COL_KV // (2 * A_KV))),
                  pl.BlockSpec((tq, IDX_Q), lambda b, i: (b * nq + i, COL_IQ // IDX_Q)),
                  pl.BlockSpec((seq, LANES), lambda b, i: (b, 0)),
                  pl.BlockSpec((tq, LANES), lambda b, i: (b * nq + i, 0)),
                  pl.BlockSpec((A_HEADS, 3 * tq, tq), lambda b, i: (0, 0, 0),
                               pipeline_mode=pl.Buffered(1))],
        out_specs=pl.BlockSpec((tq, A_Q), lambda b, i: (b * nq + i, 0)),
        out_shape=jax.ShapeDtypeStruct((batch * seq, A_Q), MXU_DTYPE),
        scratch_shapes=[pltpu.VMEM((seq, LANES), MXU_DTYPE),
                        pltpu.VMEM((seq, LANES), MXU_DTYPE),
                        pltpu.VMEM((A_KV_HEADS, nq, A_HEAD_DIM, tq), MXU_DTYPE),
                        pltpu.VMEM((LANES, tq), jnp.float32),
                        pltpu.VMEM((nq, tq, tq), jnp.int32),
                        pltpu.VMEM((nq, tq, tq), jnp.int16),
                        pltpu.VMEM((nq, tq, tq), jnp.int16),
                        pltpu.VMEM((A_HEADS, 1, tq), jnp.float32),
                        pltpu.VMEM((A_HEADS, 1, tq), jnp.float32),
                        pltpu.VMEM((A_HEADS, 1, tq), jnp.float32),
                        pltpu.VMEM((A_HEADS, A_HEAD_DIM, tq), jnp.float32),
                        pltpu.VMEM((A_HEADS, tq, tq), jnp.float32),
                        pltpu.VMEM((A_HEADS, tq, tq), jnp.float32),
                        pltpu.VMEM((A_HEADS, tq, tq), MXU_DTYPE)],
        compiler_params=_cparams("arbitrary", "arbitrary"), name="dsa")(
            proj, proj, proj, ikw, ikw, near_bias)


def _ret_kernel(qk_ref, vg_ref, cos_ref, sin_ref, gn_ref, o_ref, state_ref, dm_ref):
    c_len = RET_CHUNK
    f32 = jnp.float32
    hd = R_HEAD_DIM
    half = hd // 2
    log_g = [math.log(1.0 - 2.0 ** (-5.0 - h)) for h in range(R_HEADS)]

    @pl.when(pl.program_id(1) == 0)
    def _():
        state_ref[...] = jnp.zeros_like(state_ref)
        i = lax.broadcasted_iota(jnp.int32, (c_len, c_len), 0)
        j = lax.broadcasted_iota(jnp.int32, (c_len, c_len), 1)
        diff = (i - j).astype(f32)
        for h in range(R_HEADS):
            dm_ref[h] = jnp.where(diff >= 0, jnp.exp(log_g[h] * jnp.maximum(diff, 0.0)), 0.0)

    row = lax.broadcasted_iota(jnp.int32, (c_len, hd), 0).astype(f32)
    cos = cos_ref[...]
    sin = sin_ref[...]

    def rot(x):
        x1, x2 = x[:, :half], x[:, half:]
        return jnp.concatenate([x1 * cos - x2 * sin, x1 * sin + x2 * cos], axis=1)

    for h in range(R_HEADS):
        cols = slice(h * hd, (h + 1) * hd)
        qf = rot(qk_ref[:, cols].astype(f32))
        kf = rot(qk_ref[:, R_W + h * hd:R_W + (h + 1) * hd].astype(f32)) * (hd ** -0.5)
        v = vg_ref[:, cols]
        att = lax.dot_general(qf.astype(MXU_DTYPE), kf.astype(MXU_DTYPE), _NT,
                              preferred_element_type=f32) * dm_ref[h]
        xi = jnp.exp(log_g[h] * (row + 1.0))
        zeta = jnp.exp(log_g[h] * (c_len - 1.0 - row))
        state = state_ref[h]
        o = _dot(att.astype(MXU_DTYPE), v) + _dot((qf * xi).astype(MXU_DTYPE), state.astype(MXU_DTYPE))
        state_ref[h] = state * math.exp(log_g[h] * c_len) + lax.dot_general(
            (kf * zeta).astype(MXU_DTYPE), v, _TN, preferred_element_type=f32)
        mu = jnp.mean(o, axis=-1, keepdims=True)
        d = o - mu
        var = jnp.mean(d * d, axis=-1, keepdims=True)
        y = d * lax.rsqrt(var + GN_EPS) * gn_ref[:, cols]
        g = vg_ref[:, R_W + h * hd:R_W + (h + 1) * hd].astype(f32)
        o_ref[:, cols] = (g * jax.nn.sigmoid(g) * y).astype(o_ref.dtype)


def _rope_tables(seq):
    half = R_HEAD_DIM // 2
    inv = ROPE_BASE ** (-jnp.arange(half, dtype=jnp.float32) / half)
    ang = jnp.arange(seq, dtype=jnp.float32)[:, None] * inv[None, :]
    return jnp.cos(ang), jnp.sin(ang)


def _retention(proj, cos, sin, gn_gain, batch, seq):
    c_len = RET_CHUNK
    nc = seq // c_len
    half = R_HEAD_DIM // 2
    return pl.pallas_call(
        _ret_kernel, grid=(batch, nc),
        in_specs=[pl.BlockSpec((c_len, 2 * R_W), lambda b, c: (b * nc + c, COL_RQK // (2 * R_W))),
                  pl.BlockSpec((c_len, 2 * R_W), lambda b, c: (b * nc + c, COL_RVG // (2 * R_W))),
                  pl.BlockSpec((c_len, half), lambda b, c: (c, 0)),
                  pl.BlockSpec((c_len, half), lambda b, c: (c, 0)),
                  pl.BlockSpec((1, R_W), lambda b, c: (0, 0))],
        out_specs=pl.BlockSpec((c_len, R_W), lambda b, c: (b * nc + c, 0)),
        out_shape=jax.ShapeDtypeStruct((batch * seq, R_W), MXU_DTYPE),
        scratch_shapes=[pltpu.VMEM((R_HEADS, R_HEAD_DIM, R_HEAD_DIM), jnp.float32),
                        pltpu.VMEM((R_HEADS, c_len, c_len), jnp.float32)],
        compiler_params=_cparams("arbitrary", "arbitrary"), name="retention")(
            proj, proj, cos, sin, gn_gain.reshape(1, R_W))


def _mixer(h, norm_gain, w_in, q_gain, k_gain, gn_gain, w_out, near_bias, cos, sin, batch, seq):
    head_gains = jnp.stack([q_gain * (A_HEAD_DIM ** -0.5 * LOG2E), k_gain])
    proj, ikw = _project(h, norm_gain, _merged_in_weights(w_in), head_gains)
    a_out = _dsa(proj, ikw, near_bias, batch, seq)
    r_out = _retention(proj, cos, sin, gn_gain, batch, seq)
    return _out_project(a_out, r_out, w_out.astype(MXU_DTYPE), h)


def kernel(x, rel_bias, norm_mix, w_in, q_gain, k_gain, ret_gain, w_out, norm_ffn,
           ffn_w_gate, ffn_w_up, ffn_w_down, moe_router, moe_w_gate, moe_w_up, moe_w_down):
    batch, seq, d = x.shape
    depth = norm_mix.shape[0]
    h = x.reshape(batch * seq, d)
    near_bias = _near_bias(rel_bias)
    cos, sin = _rope_tables(seq)
    for l in range(depth):
        h = _mixer(h, norm_mix[l], w_in[l], q_gain[l], k_gain[l], ret_gain[l], w_out[l],
                   near_bias, cos, sin, batch, seq)
        j = l // 2
        if l % 2 == 0:
            hid = _gate_up(h, norm_ffn[l], ffn_w_gate[j].astype(MXU_DTYPE),
                           ffn_w_up[j].astype(MXU_DTYPE))
            f = ffn_w_down.shape[1]
            h = _down(hid, ffn_w_down[j].astype(MXU_DTYPE), h)
        else:
            h = _moe(h, norm_ffn[l], moe_router[j], moe_w_gate[j], moe_w_up[j], moe_w_down[j])
    return h.reshape(batch, seq, d)
```

```python
import functools
import math

import jax
import jax.numpy as jnp
import numpy as np
from jax import lax
from jax.experimental import pallas as pl
from jax.experimental.pallas import tpu as pltpu

CHUNK = 64
A_HEADS = 8
A_KV_HEADS = 2
A_HEAD_DIM = 128
IDX_HEADS = 16
IDX_DIM = 64
TOPK_MAX = 256
REL_BUCKETS = 32
REL_MAX_DIST = 128
R_HEADS = 4
R_HEAD_DIM = 256
ROPE_BASE = 10000.0
A_Q = A_HEADS * A_HEAD_DIM
A_KV = A_KV_HEADS * A_HEAD_DIM
IDX_Q = IDX_HEADS * IDX_DIM
R_W = R_HEADS * R_HEAD_DIM
N_EXPERTS = 8
EPS = 1e-6
GN_EPS = 1e-5

LANES = 128
SUBLANES = 8
PACKED_ROWS = 16
HALF16 = 1 << 15
LOG2E = math.log2(math.e)
VMEM_LIMIT_BYTES = 56 * 1024 * 1024
MOE_VMEM_LIMIT_BYTES = 60 * 1024 * 1024
MXU_DTYPE = jnp.bfloat16
DSA_TQ = 256
RET_CHUNK = 256
MOE_TM = 512
COMBINE_TM = 256
ROUTE_LANE = 8
INT_MIN = -2147483648
MASKED = -1e30

_NT = (((1,), (1,)), ((), ()))
_TN = (((0,), (0,)), ((), ()))


def _cparams(*sem):
    return pltpu.CompilerParams(dimension_semantics=sem, vmem_limit_bytes=VMEM_LIMIT_BYTES)


def _dot(a, b):
    return jnp.dot(a, b, preferred_element_type=jnp.float32)


def _rms(x_ref, g_ref):
    x = x_ref[...]
    ms = jnp.mean(x * x, axis=-1, keepdims=True)
    return x * lax.rsqrt(ms + EPS) * g_ref[...]


def _norm_router_kernel(x_ref, g_ref, wr_ref, o_ref, gates_ref):
    xn = _rms(x_ref, g_ref)
    o_ref[...] = xn.astype(o_ref.dtype)
    x_hi = xn.astype(jnp.bfloat16)
    x_lo = (xn - x_hi.astype(jnp.float32)).astype(jnp.bfloat16)
    w = wr_ref[...]
    w_hi = w.astype(jnp.bfloat16)
    w_lo = (w - w_hi.astype(jnp.float32)).astype(jnp.bfloat16)
    logits = _dot(x_hi, w_hi) + (_dot(x_lo, w_hi) + _dot(x_hi, w_lo))
    lane = lax.broadcasted_iota(jnp.int32, logits.shape, 1)
    neg = jnp.float32(-jnp.inf)
    logits = jnp.where(lane < N_EXPERTS, logits, neg)
    m1 = jnp.max(logits, axis=-1, keepdims=True)
    i1 = jnp.min(jnp.where(logits == m1, lane, LANES), axis=-1, keepdims=True)
    rest = jnp.where(lane == i1, neg, logits)
    m2 = jnp.max(rest, axis=-1, keepdims=True)
    i2 = jnp.min(jnp.where(rest == m2, lane, LANES), axis=-1, keepdims=True)
    e2 = jnp.exp(m2 - m1)
    g1 = 1.0 / (1.0 + e2)
    g2 = e2 / (1.0 + e2)
    route = jnp.where(lane == ROUTE_LANE, g1, 0.0) + jnp.where(lane == ROUTE_LANE + 1, g2, 0.0)
    route = route + jnp.where(lane == ROUTE_LANE + 2, i1.astype(jnp.float32), 0.0)
    route = route + jnp.where(lane == ROUTE_LANE + 3, i2.astype(jnp.float32), 0.0)
    gates_ref[...] = route


def _rmsnorm_router(x, gain, router_w, tm=512):
    n, d = x.shape
    wr = jnp.pad(router_w, ((0, 0), (0, LANES - router_w.shape[1])))
    return pl.pallas_call(
        _norm_router_kernel, grid=(n // tm,),
        in_specs=[pl.BlockSpec((tm, d), lambda i: (i, 0)),
                  pl.BlockSpec((1, d), lambda i: (0, 0)),
                  pl.BlockSpec((d, LANES), lambda i: (0, 0))],
        out_specs=[pl.BlockSpec((tm, d), lambda i: (i, 0)),
                   pl.BlockSpec((tm, LANES), lambda i: (i, 0))],
        out_shape=[jax.ShapeDtypeStruct((n, d), jnp.float32),
                   jax.ShapeDtypeStruct((n, LANES), jnp.float32)],
        compiler_params=_cparams("parallel"), name="rmsnorm_router")(x, gain.reshape(1, d), wr)


PROJ_TN = 512
COL_RQK = 0
COL_RVG = COL_RQK + 2 * R_W
COL_Q = COL_RVG + 2 * R_W
COL_IQ = COL_Q + A_Q
COL_KV = COL_IQ + IDX_Q
COL_IKW = COL_KV + 2 * A_KV
PROJ_WIDTH = COL_IKW + PROJ_TN


def _proj_kernel(x_ref, g_ref, w_ref, hg_ref, o_ref, ikw_ref, xn_ref):
    j = pl.program_id(1)

    @pl.when(j == 0)
    def _():
        xn_ref[...] = _rms(x_ref, g_ref).astype(xn_ref.dtype)

    acc = _dot(xn_ref[...], w_ref[...])
    o_ref[...] = acc.astype(o_ref.dtype)

    def renorm(n_heads, gain_row):
        for h in range(n_heads):
            y = acc[:, h * A_HEAD_DIM:(h + 1) * A_HEAD_DIM]
            ms = jnp.mean(y * y, axis=-1, keepdims=True)
            y = y * lax.rsqrt(ms + EPS) * hg_ref[gain_row:gain_row + 1, :]
            o_ref[:, h * A_HEAD_DIM:(h + 1) * A_HEAD_DIM] = y.astype(o_ref.dtype)

    @pl.when(jnp.logical_and(j >= COL_Q // PROJ_TN, j < COL_IQ // PROJ_TN))
    def _():
        renorm(PROJ_TN // A_HEAD_DIM, 0)

    @pl.when(j == COL_KV // PROJ_TN)
    def _():
        renorm(A_KV // A_HEAD_DIM, 1)

    @pl.when(j == COL_IKW // PROJ_TN)
    def _():
        ikw_ref[...] = acc[:, :LANES]


def _project(x, norm_gain, w_all, head_gains, tm=1024):
    n, d = x.shape
    return pl.pallas_call(
        _proj_kernel, grid=(n // tm, PROJ_WIDTH // PROJ_TN),
        in_specs=[pl.BlockSpec((tm, d), lambda i, j: (i, 0)),
                  pl.BlockSpec((1, d), lambda i, j: (0, 0)),
                  pl.BlockSpec((d, PROJ_TN), lambda i, j: (0, j)),
                  pl.BlockSpec((2, A_HEAD_DIM), lambda i, j: (0, 0))],
        out_specs=[pl.BlockSpec((tm, PROJ_TN), lambda i, j: (i, j)),
                   pl.BlockSpec((tm, LANES), lambda i, j: (i, 0))],
        out_shape=[jax.ShapeDtypeStruct((n, PROJ_WIDTH), MXU_DTYPE),
                   jax.ShapeDtypeStruct((n, LANES), jnp.float32)],
        scratch_shapes=[pltpu.VMEM((tm, d), MXU_DTYPE)],
        compiler_params=_cparams("parallel", "arbitrary"), name="in_proj")(
            x, norm_gain.reshape(1, d), w_all, head_gains)


def _merged_in_weights(w_in):
    c0 = A_Q
    c1 = c0 + 2 * A_KV
    c2 = c1 + IDX_Q
    c3 = c2 + IDX_DIM + IDX_HEADS
    c4 = c3 + 2 * R_W
    ikw = jnp.pad(w_in[:, c2:c3], ((0, 0), (0, PROJ_TN - (c3 - c2))))
    parts = [w_in[:, c3:c4], w_in[:, c4:], w_in[:, :c0], w_in[:, c1:c2], w_in[:, c0:c1], ikw]
    return jnp.concatenate(parts, axis=1).astype(MXU_DTYPE)


def _outproj_kernel(a_ref, r_ref, wa_ref, wr_ref, res_ref, o_ref):
    o_ref[...] = res_ref[...] + (_dot(a_ref[...], wa_ref[...]) + _dot(r_ref[...], wr_ref[...]))


def _out_project(a, r, w, res, tm=512, tn=2048):
    n, ka = a.shape
    kr = r.shape[1]
    assert ka == kr
    d = w.shape[1]
    tn = min(tn, d)
    return pl.pallas_call(
        _outproj_kernel, grid=(n // tm, d // tn),
        in_specs=[pl.BlockSpec((tm, ka), lambda i, j: (i, 0)),
                  pl.BlockSpec((tm, kr), lambda i, j: (i, 0)),
                  pl.BlockSpec((ka, tn), lambda i, j: (0, j)),
                  pl.BlockSpec((kr, tn), lambda i, j: (1, j)),
                  pl.BlockSpec((tm, tn), lambda i, j: (i, j))],
        out_specs=pl.BlockSpec((tm, tn), lambda i, j: (i, j)),
        out_shape=jax.ShapeDtypeStruct((n, d), jnp.float32),
        compiler_params=_cparams("parallel", "parallel"), name="out_proj")(a, r, w, w, res)


def _swiglu_hidden(x, wg, wu):
    g = _dot(x, wg)
    u = _dot(x, wu)
    return g * jax.nn.sigmoid(g) * u


def _gu_kernel(x_ref, g_ref, wg_ref, wu_ref, o_ref, xn_ref):
    @pl.when(pl.program_id(1) == 0)
    def _():
        xn_ref[...] = _rms(x_ref, g_ref).astype(xn_ref.dtype)

    o_ref[...] = _swiglu_hidden(xn_ref[...], wg_ref[...], wu_ref[...]).astype(o_ref.dtype)


def _gate_up(x, norm_gain, wg, wu, tm=1024, tn=512):
    n, d = x.shape
    f = wg.shape[1]
    return pl.pallas_call(
        _gu_kernel, grid=(n // tm, f // tn),
        in_specs=[pl.BlockSpec((tm, d), lambda i, j: (i, 0)),
                  pl.BlockSpec((1, d), lambda i, j: (0, 0)),
                  pl.BlockSpec((d, tn), lambda i, j: (0, j)),
                  pl.BlockSpec((d, tn), lambda i, j: (0, j))],
        out_specs=pl.BlockSpec((tm, tn), lambda i, j: (i, j)),
        out_shape=jax.ShapeDtypeStruct((n, f), MXU_DTYPE),
        scratch_shapes=[pltpu.VMEM((tm, d), MXU_DTYPE)],
        compiler_params=_cparams("parallel", "arbitrary"), name="gate_up")(
            x, norm_gain.reshape(1, d), wg, wu)


def _down_kernel(x_ref, w_ref, res_ref, o_ref):
    o_ref[...] = res_ref[...] + _dot(x_ref[...], w_ref[...])


def _down(x, w, res, tm=512, tn=1024):
    n, kt = x.shape
    d = w.shape[1]
    tn = min(tn, d)
    return pl.pallas_call(
        _down_kernel, grid=(d // tn, n // tm),
        in_specs=[pl.BlockSpec((tm, kt), lambda j, i: (i, 0)),
                  pl.BlockSpec((kt, tn), lambda j, i: (0, j)),
                  pl.BlockSpec((tm, tn), lambda j, i: (i, j))],
        out_specs=pl.BlockSpec((tm, tn), lambda j, i: (i, j)),
        out_shape=jax.ShapeDtypeStruct((n, d), jnp.float32),
        compiler_params=_cparams("parallel", "parallel"), name="down_proj")(x, w, res)


def _route_tables(route, tm):
    n = route.shape[0]
    e = route[:, ROUTE_LANE + 2:ROUTE_LANE + 4].astype(jnp.int32).reshape(-1)
    onehot = (e[:, None] == jnp.arange(N_EXPERTS, dtype=jnp.int32)[None, :]).astype(jnp.int32)
    csum = jnp.cumsum(onehot, axis=0)
    rank = jnp.sum(csum * onehot, axis=1) - 1
    tiles_per = (csum[-1] + tm - 1) // tm
    tile_end = jnp.cumsum(tiles_per)
    row_start = (tile_end - tiles_per) * tm
    pos = jnp.sum(onehot * row_start[None, :], axis=1) + rank
    n_tiles = (2 * n) // tm + N_EXPERTS
    t = jnp.arange(n_tiles, dtype=jnp.int32)
    tile_expert = jnp.sum((t[:, None] >= tile_end[None, :]).astype(jnp.int32), axis=1)
    meta = jnp.concatenate([jnp.minimum(tile_expert, N_EXPERTS - 1), tile_end[-1:]]).astype(jnp.int32)
    src = jnp.zeros((n_tiles * tm,), jnp.int32).at[pos].set(
        jnp.arange(2 * n, dtype=jnp.int32) // 2, unique_indices=True)
    return src.reshape(n_tiles, 1, tm), pos.reshape(n, 2), meta


def _row_copy(src_hbm, row, dst, slot, r, sem):
    return pltpu.make_async_copy(src_hbm.at[pl.ds(row, 1)], dst.at[slot, pl.ds(r, 1)], sem)


def _gather_kernel(idx_ref, nxt_ref, x_hbm, o_ref, buf, sem):
    i = pl.program_id(0)
    tm = buf.shape[1]
    slot = lax.rem(i, 2)

    def issue(ref, s):
        def body(k, c):
            for u in range(2):
                r = 2 * k + u
                _row_copy(x_hbm, ref[0, 0, r], buf, s, r, sem.at[s]).start(priority=u)
            return c
        lax.fori_loop(0, tm // 2, body, 0, unroll=4)

    @pl.when(i == 0)
    def _():
        issue(idx_ref, 0)

    @pl.when(i + 1 < pl.num_programs(0))
    def _():
        issue(nxt_ref, 1 - slot)

    pltpu.make_async_copy(x_hbm.at[pl.ds(0, tm)], buf.at[slot], sem.at[slot]).wait()
    o_ref[...] = buf[slot].astype(o_ref.dtype)


def _moe_gather(x, src):
    n_tiles, _, tm = src.shape
    d = x.shape[1]
    idx_spec = pl.BlockSpec((1, 1, tm), lambda i: (i, 0, 0), memory_space=pltpu.SMEM)
    nxt_spec = pl.BlockSpec((1, 1, tm), lambda i: (jnp.minimum(i + 1, n_tiles - 1), 0, 0),
                            memory_space=pltpu.SMEM)
    return pl.pallas_call(
        _gather_kernel, grid=(n_tiles,),
        in_specs=[idx_spec, nxt_spec, pl.BlockSpec(memory_space=pl.ANY)],
        out_specs=pl.BlockSpec((tm, d), lambda i: (i, 0)),
        out_shape=jax.ShapeDtypeStruct((n_tiles * tm, d), MXU_DTYPE),
        scratch_shapes=[pltpu.VMEM((2, tm, d), x.dtype), pltpu.SemaphoreType.DMA((2,))],
        compiler_params=_cparams("arbitrary"), name="moe_gather")(src, src, x)


def _moe_gu_kernel(meta_ref, x_ref, wg_ref, wu_ref, o_ref, wgb_ref, wub_ref):
    i = pl.program_id(1)
    used = i < meta_ref[pl.num_programs(1)]
    new_block = jnp.logical_or(i == 0, meta_ref[i] != meta_ref[jnp.maximum(i - 1, 0)])

    @pl.when(jnp.logical_and(used, new_block))
    def _():
        wgb_ref[...] = wg_ref[0].astype(wgb_ref.dtype)
        wub_ref[...] = wu_ref[0].astype(wub_ref.dtype)

    @pl.when(used)
    def _():
        o_ref[...] = _swiglu_hidden(x_ref[...], wgb_ref[...], wub_ref[...]).astype(o_ref.dtype)

    @pl.when(jnp.logical_not(used))
    def _():
        o_ref[...] = jnp.zeros_like(o_ref)


def _moe_gate_up(xs, wg, wu, meta, tm, tn=1024):
    p, d = xs.shape
    f = wg.shape[2]
    tn = min(tn, f)
    grid_spec = pltpu.PrefetchScalarGridSpec(
        num_scalar_prefetch=1, grid=(f // tn, p // tm),
        in_specs=[pl.BlockSpec((tm, d), lambda j, i, m: (i, 0)),
                  pl.BlockSpec((1, d, tn), lambda j, i, m: (m[i], 0, j)),
                  pl.BlockSpec((1, d, tn), lambda j, i, m: (m[i], 0, j))],
        out_specs=pl.BlockSpec((tm, tn), lambda j, i, m: (i, j)),
        scratch_shapes=[pltpu.VMEM((d, tn), MXU_DTYPE), pltpu.VMEM((d, tn), MXU_DTYPE)])
    return pl.pallas_call(
        _moe_gu_kernel, grid_spec=grid_spec,
        out_shape=jax.ShapeDtypeStruct((p, f), MXU_DTYPE),
        compiler_params=pltpu.CompilerParams(dimension_semantics=("arbitrary", "arbitrary"),
                                             vmem_limit_bytes=MOE_VMEM_LIMIT_BYTES),
        name="moe_gate_up")(meta, xs, wg, wu)


def _moe_down_kernel(meta_ref, x_ref, w_ref, o_ref, wb_ref):
    i = pl.program_id(1)
    used = i < meta_ref[pl.num_programs(1)]
    new_block = jnp.logical_or(i == 0, meta_ref[i] != meta_ref[jnp.maximum(i - 1, 0)])

    @pl.when(jnp.logical_and(used, new_block))
    def _():
        wb_ref[...] = w_ref[0].astype(wb_ref.dtype)

    @pl.when(used)
    def _():
        o_ref[...] = _dot(x_ref[...], wb_ref[...])

    @pl.when(jnp.logical_not(used))
    def _():
        o_ref[...] = jnp.zeros_like(o_ref)


def _moe_down(hid, wd, meta, tm, tn=512):
    p, f = hid.shape
    d = wd.shape[2]
    tn = min(tn, d)
    grid_spec = pltpu.PrefetchScalarGridSpec(
        num_scalar_prefetch=1, grid=(d // tn, p // tm),
        in_specs=[pl.BlockSpec((tm, f), lambda j, i, m: (i, 0)),
                  pl.BlockSpec((1, f, tn), lambda j, i, m: (m[i], 0, j))],
        out_specs=pl.BlockSpec((tm, tn), lambda j, i, m: (i, j)),
        scratch_shapes=[pltpu.VMEM((f, tn), MXU_DTYPE)])
    return pl.pallas_call(
        _moe_down_kernel, grid_spec=grid_spec,
        out_shape=jax.ShapeDtypeStruct((p, d), jnp.float32),
        compiler_params=pltpu.CompilerParams(dimension_semantics=("arbitrary", "arbitrary"),
                                             vmem_limit_bytes=MOE_VMEM_LIMIT_BYTES),
        name="moe_down")(meta, hid, wd)


def _combine_kernel(pa_ref, pb_ref, pa_nxt, pb_nxt, y_hbm, h_ref, route_ref, o_ref, bufa, bufb, sem):
    i = pl.program_id(0)
    tm = bufa.shape[1]
    slot = lax.rem(i, 2)

    def issue(pa, pb, s):
        def body(r, c):
            _row_copy(y_hbm, pa[0, 0, r], bufa, s, r, sem.at[0, s]).start(priority=0)
            _row_copy(y_hbm, pb[0, 0, r], bufb, s, r, sem.at[1, s]).start(priority=1)
            return c
        lax.fori_loop(0, tm, body, 0, unroll=4)

    @pl.when(i == 0)
    def _():
        issue(pa_ref, pb_ref, 0)

    @pl.when(i + 1 < pl.num_programs(0))
    def _():
        issue(pa_nxt, pb_nxt, 1 - slot)

    pltpu.make_async_copy(y_hbm.at[pl.ds(0, tm)], bufa.at[slot], sem.at[0, slot]).wait()
    pltpu.make_async_copy(y_hbm.at[pl.ds(0, tm)], bufb.at[slot], sem.at[1, slot]).wait()
    route = route_ref[...]
    ga = route[:, ROUTE_LANE:ROUTE_LANE + 1]
    gb = route[:, ROUTE_LANE + 1:ROUTE_LANE + 2]
    o_ref[...] = h_ref[...] + (ga * bufa[slot] + gb * bufb[slot])


def _moe_combine(y, pos, h, route, tm):
    n, d = h.shape
    nt = n // tm
    pa = pos[:, 0].reshape(nt, 1, tm)
    pb = pos[:, 1].reshape(nt, 1, tm)
    cur = pl.BlockSpec((1, 1, tm), lambda i: (i, 0, 0), memory_space=pltpu.SMEM)
    nxt = pl.BlockSpec((1, 1, tm), lambda i: (jnp.minimum(i + 1, nt - 1), 0, 0), memory_space=pltpu.SMEM)
    return pl.pallas_call(
        _combine_kernel, grid=(nt,),
        in_specs=[cur, cur, nxt, nxt, pl.BlockSpec(memory_space=pl.ANY),
                  pl.BlockSpec((tm, d), lambda i: (i, 0)),
                  pl.BlockSpec((tm, LANES), lambda i: (i, 0))],
        out_specs=pl.BlockSpec((tm, d), lambda i: (i, 0)),
        out_shape=jax.ShapeDtypeStruct((n, d), jnp.float32),
        scratch_shapes=[pltpu.VMEM((2, tm, d), jnp.float32), pltpu.VMEM((2, tm, d), jnp.float32),
                        pltpu.SemaphoreType.DMA((2, 2))],
        compiler_params=_cparams("arbitrary"), name="moe_combine")(pa, pb, pa, pb, y, h, route)


def _moe(h, norm_gain, router_w, wg, wu, wd):
    hn, route = _rmsnorm_router(h, norm_gain, router_w)
    src, pos, meta = _route_tables(route, MOE_TM)
    xs = _moe_gather(hn, src)
    hid = _moe_gate_up(xs, wg, wu, meta, MOE_TM)
    y = _moe_down(hid, wd, meta, MOE_TM)
    return _moe_combine(y, pos, h, route, COMBINE_TM)


def _sortable(x):
    x = jnp.where(x == 0.0, 0.0, x)
    i = lax.bitcast_convert_type(x, jnp.int32)
    return i ^ ((i >> 31) & jnp.int32(0x7FFFFFFF))


def _dsa_kernel(q_ref, kv_ref, iq_ref, ikk_ref, ikq_ref, nb_ref, o_ref,
                iklo_ref, ikhi_ref, vt_ref, wt_ref, skey_ref, hi_ref, lo_ref, m_ref, l_ref, a_ref,
                acc_ref, s0_ref, s1_ref, p_ref, *, n_top):
    tq = DSA_TQ
    half_keys = tq // 2
    qi = pl.program_id(1)
    f32 = jnp.float32
    nkb = skey_ref.shape[0]

    @pl.when(qi == 0)
    def _():
        ik = ikk_ref[...]
        lane = lax.broadcasted_iota(jnp.int32, ik.shape, 1)
        iklo_ref[...] = jnp.where(lane < IDX_DIM, ik, 0.0).astype(MXU_DTYPE)
        shifted = pltpu.roll(ik, IDX_DIM, axis=1)
        ikhi_ref[...] = jnp.where(lane >= IDX_DIM, shifted, 0.0).astype(MXU_DTYPE)
        for kb in range(nkb):
            for g in range(A_KV_HEADS):
                v = kv_ref[kb * tq:(kb + 1) * tq, A_KV + g * A_HEAD_DIM:A_KV + (g + 1) * A_HEAD_DIM]
                vt_ref[g, kb] = v.astype(f32).T.astype(MXU_DTYPE)

    wt_ref[...] = ikq_ref[...].T

    key_iota = lax.broadcasted_iota(jnp.int32, (half_keys, tq), 0)
    t_chunk = (qi * tq + lax.broadcasted_iota(jnp.int32, (half_keys, tq), 1)) // CHUNK

    def score_block(kb, carry):
        for half in range(2):
            r = pl.multiple_of(kb * tq + half * half_keys, half_keys)
            klo = iklo_ref[pl.ds(r, half_keys), :]
            khi = ikhi_ref[pl.ds(r, half_keys), :]
            acc = jnp.zeros((half_keys, tq), f32)
            for p in range(IDX_HEADS // 2):
                rhs = iq_ref[:, p * LANES:(p + 1) * LANES]
                for lhs, h in ((klo, 2 * p), (khi, 2 * p + 1)):
                    hs = lax.dot_general(lhs, rhs, _NT, preferred_element_type=f32)
                    acc = acc + jnp.maximum(hs, 0.0) * wt_ref[IDX_DIM + h:IDX_DIM + h + 1, :]
            key = _sortable(acc)
            s_chunk = (r + key_iota) // CHUNK
            rows = slice(half * half_keys, (half + 1) * half_keys)
            key = jnp.where(s_chunk <= t_chunk, key, INT_MIN)
            skey_ref[kb, rows, :] = key
            hi_ref[kb, rows, :] = (key >> 16).astype(jnp.int16)
        return carry

    lax.fori_loop(0, qi + 1, score_block, 0)

    n_groups = tq // SUBLANES
    n_acc = 4
    sub_iota = lax.broadcasted_iota(jnp.int32, (SUBLANES, tq), 0)

    def rows8(x):
        return jnp.broadcast_to(x, (SUBLANES, tq))

    def count_where(pred):
        def body(kb, accs):
            accs = list(accs)
            for g in range(n_groups):
                keys = skey_ref[kb, g * SUBLANES:(g + 1) * SUBLANES, :]
                hit = jnp.where(pred(keys, kb * tq + g * SUBLANES), 1.0, 0.0)
                accs[g % n_acc] = accs[g % n_acc] + hit
            return tuple(accs)

        accs = lax.fori_loop(0, qi + 1, body,
                             tuple(jnp.zeros((SUBLANES, tq), f32) for _ in range(n_acc)))
        return jnp.sum((accs[0] + accs[1]) + (accs[2] + accs[3]), axis=0, keepdims=True)

    i16 = jnp.int16
    n_groups16 = tq // PACKED_ROWS

    def rows16(x):
        return jnp.broadcast_to(x, (PACKED_ROWS, tq))

    def count16(ref, pred):
        def body(kb, accs):
            accs = list(accs)
            for g in range(n_groups16):
                rows = ref[kb, g * PACKED_ROWS:(g + 1) * PACKED_ROWS, :]
                hit = jnp.where(pred(rows), i16(1), i16(0))
                accs[g % n_acc] = accs[g % n_acc] + hit
            return tuple(accs)

        accs = lax.fori_loop(0, qi + 1, body,
                             tuple(jnp.zeros((PACKED_ROWS, tq), i16) for _ in range(n_acc)))
        tot = (accs[0] + accs[1]) + (accs[2] + accs[3])
        return jnp.sum(tot.astype(jnp.int32), axis=0, keepdims=True)

    def search16(ref, target):
        def step(it, tu):
            bit = jnp.left_shift(jnp.int32(1), 15 - it)
            cand = rows16(((tu | bit) - HALF16).astype(i16))
            cnt = count16(ref, lambda rows: rows >= cand)
            return jnp.where(cnt >= target, tu | bit, tu)

        return lax.fori_loop(0, 16, step, jnp.zeros((1, tq), jnp.int32))

    t_hi = search16(hi_ref, n_top) - HALF16
    t_hi16 = rows16(t_hi.astype(i16))
    above = count16(hi_ref, lambda rows: rows > t_hi16)
    t_hi8 = rows8(t_hi)

    def low_block(kb, carry):
        for g in range(n_groups16):
            rows = slice(g * PACKED_ROWS, (g + 1) * PACKED_ROWS)
            keys = skey_ref[kb, rows, :]
            t2 = jnp.concatenate([t_hi8, t_hi8], axis=0)
            low = jnp.where((keys >> 16) == t2, (keys & 0xFFFF) - HALF16, -HALF16)
            lo_ref[kb, rows, :] = low.astype(i16)
        return carry

    lax.fori_loop(0, qi + 1, low_block, 0)
    t_lo = search16(lo_ref, n_top - above)
    thr = rows8(t_hi * (2 * HALF16) + t_lo)

    n_gt = count_where(lambda keys, k0: keys > thr)
    n_eq = count_where(lambda keys, k0: keys == thr)
    need = n_top - n_gt
    n_bits = max(1, int(math.ceil(math.log2(nkb * tq + 1))))

    def tie_search():
        def jstep(it, jb):
            bit = jnp.left_shift(jnp.int32(1), n_bits - 1 - it)
            bound = rows8(jb | bit)
            cnt = count_where(lambda keys, k0: (keys == thr) & ((k0 + sub_iota) < bound))
            return jnp.where(cnt <= need, jb | bit, jb)

        return lax.fori_loop(0, n_bits, jstep, jnp.zeros((1, tq), jnp.int32))

    jb = lax.cond(jnp.max(n_eq - need) > 0.0, tie_search,
                  lambda: jnp.full((1, tq), 2 ** 30, jnp.int32))
    bound = rows8(jb)

    def mask_block(kb, carry):
        for g in range(n_groups):
            rows = slice(g * SUBLANES, (g + 1) * SUBLANES)
            keys = skey_ref[kb, rows, :]
            tie = (keys == thr) & ((kb * tq + g * SUBLANES + sub_iota) < bound)
            sel = ((keys > thr) | tie) & (keys != INT_MIN)
            skey_ref[kb, rows, :] = lax.bitcast_convert_type(
                jnp.where(sel, 0.0, MASKED).astype(f32), jnp.int32)
        return carry

    lax.fori_loop(0, qi + 1, mask_block, 0)

    m_ref[...] = jnp.full(m_ref.shape, MASKED, f32)
    l_ref[...] = jnp.zeros_like(l_ref)
    acc_ref[...] = jnp.zeros_like(acc_ref)
    grp = A_HEADS // A_KV_HEADS

    def logits(kb_raw, s_ref):
        kb = jnp.minimum(kb_raw, qi)
        r = pl.multiple_of(kb * tq, tq)
        nb_row = pl.multiple_of(jnp.clip(kb - (qi - 2), 0, 2) * tq, tq)
        beyond = jnp.where(kb_raw > qi, MASKED, 0.0).astype(f32)
        mb = lax.bitcast_convert_type(skey_ref[kb], f32) + beyond
        for h in range(A_HEADS):
            g = h // grp
            kblk = kv_ref[pl.ds(r, tq), g * A_HEAD_DIM:(g + 1) * A_HEAD_DIM]
            qh = q_ref[:, h * A_HEAD_DIM:(h + 1) * A_HEAD_DIM]
            s = lax.dot_general(kblk, qh, _NT, preferred_element_type=f32)
            s_ref[h] = s + (mb + nb_ref[h, pl.ds(nb_row, tq), :])

    def softmax_pv(kb_raw, s_ref):
        kb = jnp.minimum(kb_raw, qi)
        for h in range(A_HEADS):
            m_prev = m_ref[h]
            m_new = jnp.maximum(m_prev, jnp.max(s_ref[h], axis=0, keepdims=True))
            alpha = jnp.exp2(m_prev - m_new)
            p = jnp.exp2(s_ref[h] - m_new)
            l_ref[h] = alpha * l_ref[h] + jnp.sum(p, axis=0, keepdims=True)
            m_ref[h] = m_new
            a_ref[h] = alpha
            p_ref[h] = p.astype(MXU_DTYPE)
        for h in range(A_HEADS):
            vt = vt_ref[h // grp, kb]
            acc_ref[h] = a_ref[h] * acc_ref[h] + _dot(vt, p_ref[h])

    def attend_pair(j, carry):
        logits(2 * j + 1, s1_ref)
        softmax_pv(2 * j, s0_ref)
        logits(2 * j + 2, s0_ref)
        softmax_pv(2 * j + 1, s1_ref)
        return carry

    logits(0, s0_ref)
    lax.fori_loop(0, (qi + 2) // 2, attend_pair, 0)

    for h in range(A_HEADS):
        o = (acc_ref[h] / l_ref[h]).T
        o_ref[:, h * A_HEAD_DIM:(h + 1) * A_HEAD_DIM] = o.astype(o_ref.dtype)


def _t5_bucket(rel):
    half = REL_BUCKETS // 2
    max_exact = half // 2
    ret = jnp.where(rel > 0, half, 0)
    n = jnp.abs(rel)
    nf = jnp.maximum(n, 1).astype(jnp.float32)
    large = max_exact + (jnp.log(nf / max_exact) / math.log(REL_MAX_DIST / max_exact)
                         * (half - max_exact)).astype(jnp.int32)
    large = jnp.minimum(large, half - 1)
    return ret + jnp.where(n < max_exact, n, large)


def _near_bias(rel_bias):
    tq = DSA_TQ
    c = jnp.arange(2 * tq, dtype=jnp.int32)[:, None]
    i = jnp.arange(tq, dtype=jnp.int32)[None, :]
    rel = c - tq - i
    far = rel_bias[_t5_bucket(jnp.full((1,), -REL_MAX_DIST, jnp.int32))]
    onehot = (_t5_bucket(rel)[None] == jnp.arange(REL_BUCKETS, dtype=jnp.int32)[:, None, None])
    table = (rel_bias - far).T * LOG2E
    near = jnp.einsum('hb,bci->hci', table, onehot.astype(jnp.float32),
                      precision=lax.Precision.HIGHEST)
    return jnp.pad(near, ((0, 0), (tq, 0), (0, 0)))


def _dsa(proj, ikw, near_bias, batch, seq):
    tq = DSA_TQ
    nq = seq // tq
    n_top = min(TOPK_MAX, seq // 4)
    return pl.pallas_call(
        functools.partial(_dsa_kernel, n_top=n_top),
        grid=(batch, nq),
        in_specs=[pl.BlockSpec((tq, A_Q), lambda b, i: (b * nq + i, COL_Q // A_Q)),
                  pl.BlockSpec((seq, 2 * A_KV), lambda b, i: (b, COL_KV // (2 * A_KV))),
                  pl.BlockSpec((tq, IDX_Q), lambda b, i: (b * nq + i, COL_IQ // IDX_Q)),
                  pl.BlockSpec((seq, LANES), lambda b, i: (b, 0)),
                  pl.BlockSpec((tq, LANES), lambda b, i: (b * nq + i, 0)),
                  pl.BlockSpec((A_HEADS, 3 * tq, tq), lambda b, i: (0, 0, 0),
                               pipeline_mode=pl.Buffered(1))],
        out_specs=pl.BlockSpec((tq, A_Q), lambda b, i: (b * nq + i, 0)),
        out_shape=jax.ShapeDtypeStruct((batch * seq, A_Q), MXU_DTYPE),
        scratch_shapes=[pltpu.VMEM((seq, LANES), MXU_DTYPE),
                        pltpu.VMEM((seq, LANES), MXU_DTYPE),
                        pltpu.VMEM((A_KV_HEADS, nq, A_HEAD_DIM, tq), MXU_DTYPE),
                        pltpu.VMEM((LANES, tq), jnp.float32),
                        pltpu.VMEM((nq, tq, tq), jnp.int32),
                        pltpu.VMEM((nq, tq, tq), jnp.int16),
                        pltpu.VMEM((nq, tq, tq), jnp.int16),
                        pltpu.VMEM((A_HEADS, 1, tq), jnp.float32),
                        pltpu.VMEM((A_HEADS, 1, tq), jnp.float32),
                        pltpu.VMEM((A_HEADS, 1, tq), jnp.float32),
                        pltpu.VMEM((A_HEADS, A_HEAD_DIM, tq), jnp.float32),
                        pltpu.VMEM((A_HEADS, tq, tq), jnp.float32),
                        pltpu.VMEM((A_HEADS, tq, tq), jnp.float32),
                        pltpu.VMEM((A_HEADS, tq, tq), MXU_DTYPE)],
        compiler_params=_cparams("arbitrary", "arbitrary"), name="dsa")(
            proj, proj, proj, ikw, ikw, near_bias)


def _ret_kernel(qk_ref, vg_ref, cos_ref, sin_ref, gn_ref, o_ref, state_ref, dm_ref):
    c_len = RET_CHUNK
    f32 = jnp.float32
    hd = R_HEAD_DIM
    half = hd // 2
    log_g = [math.log(1.0 - 2.0 ** (-5.0 - h)) for h in range(R_HEADS)]

    @pl.when(pl.program_id(1) == 0)
    def _():
        state_ref[...] = jnp.zeros_like(state_ref)
        i = lax.broadcasted_iota(jnp.int32, (c_len, c_len), 0)
        j = lax.broadcasted_iota(jnp.int32, (c_len, c_len), 1)
        diff = (i - j).astype(f32)
        for h in range(R_HEADS):
            dm_ref[h] = jnp.where(diff >= 0, jnp.exp(log_g[h] * jnp.maximum(diff, 0.0)), 0.0)

    row = lax.broadcasted_iota(jnp.int32, (c_len, hd), 0).astype(f32)
    cos = cos_ref[...]
    sin = sin_ref[...]

    def rot(x):
        x1, x2 = x[:, :half], x[:, half:]
        return jnp.concatenate([x1 * cos - x2 * sin, x1 * sin + x2 * cos], axis=1)

    for h in range(R_HEADS):
        cols = slice(h * hd, (h + 1) * hd)
        qf = rot(qk_ref[:, cols].astype(f32))
        kf = rot(qk_ref[:, R_W + h * hd:R_W + (h + 1) * hd].astype(f32)) * (hd ** -0.5)
        v = vg_ref[:, cols]
        att = lax.dot_general(qf.astype(MXU_DTYPE), kf.astype(MXU_DTYPE), _NT,
                              preferred_element_type=f32) * dm_ref[h]
        xi = jnp.exp(log_g[h] * (row + 1.0))
        zeta = jnp.exp(log_g[h] * (c_len - 1.0 - row))
        state = state_ref[h]
        o = _dot(att.astype(MXU_DTYPE), v) + _dot((qf * xi).astype(MXU_DTYPE), state.astype(MXU_DTYPE))
        state_ref[h] = state * math.exp(log_g[h] * c_len) + lax.dot_general(
            (kf * zeta).astype(MXU_DTYPE), v, _TN, preferred_element_type=f32)
        mu = jnp.mean(o, axis=-1, keepdims=True)
        d = o - mu
        var = jnp.mean(d * d, axis=-1, keepdims=True)
        y = d * lax.rsqrt(var + GN_EPS) * gn_ref[:, cols]
        g = vg_ref[:, R_W + h * hd:R_W + (h + 1) * hd].astype(f32)
        o_ref[:, cols] = (g * jax.nn.sigmoid(g) * y).astype(o_ref.dtype)


def _rope_tables(seq):
    half = R_HEAD_DIM // 2
    inv = ROPE_BASE ** (-jnp.arange(half, dtype=jnp.float32) / half)
    ang = jnp.arange(seq, dtype=jnp.float32)[:, None] * inv[None, :]
    return jnp.cos(ang), jnp.sin(ang)


def _retention(proj, cos, sin, gn_gain, batch, seq):
    c_len = RET_CHUNK
    nc = seq // c_len
    half = R_HEAD_DIM // 2
    return pl.pallas_call(
        _ret_kernel, grid=(batch, nc),
        in_specs=[pl.BlockSpec((c_len, 2 * R_W), lambda b, c: (b * nc + c, COL_RQK // (2 * R_W))),
                  pl.BlockSpec((c_len, 2 * R_W), lambda b, c: (b * nc + c, COL_RVG // (2 * R_W))),
                  pl.BlockSpec((c_len, half), lambda b, c: (c, 0)),
                  pl.BlockSpec((c_len, half), lambda b, c: (c, 0)),
                  pl.BlockSpec((1, R_W), lambda b, c: (0, 0))],
        out_specs=pl.BlockSpec((c_len, R_W), lambda b, c: (b * nc + c, 0)),
        out_shape=jax.ShapeDtypeStruct((batch * seq, R_W), MXU_DTYPE),
        scratch_shapes=[pltpu.VMEM((R_HEADS, R_HEAD_DIM, R_HEAD_DIM), jnp.float32),
                        pltpu.VMEM((R_HEADS, c_len, c_len), jnp.float32)],
        compiler_params=_cparams("arbitrary", "arbitrary"), name="retention")(
            proj, proj, cos, sin, gn_gain.reshape(1, R_W))


def _mixer(h, norm_gain, w_in, q_gain, k_gain, gn_gain, w_out, near_bias, cos, sin, batch, seq):
    head_gains = jnp.stack([q_gain * (A_HEAD_DIM ** -0.5 * LOG2E), k_gain])
    proj, ikw = _project(h, norm_gain, _merged_in_weights(w_in), head_gains)
    a_out = _dsa(proj, ikw, near_bias, batch, seq)
    r_out = _retention(proj, cos, sin, gn_gain, batch, seq)
    return _out_project(a_out, r_out, w_out.astype(MXU_DTYPE), h)


def kernel(x, rel_bias, norm_mix, w_in, q_gain, k_gain, ret_gain, w_out, norm_ffn,
           ffn_w_gate, ffn_w_up, ffn_w_down, moe_router, moe_w_gate, moe_w_up, moe_w_down):
    batch, seq, d = x.shape
    depth = norm_mix.shape[0]
    h = x.reshape(batch * seq, d)
    near_bias = _near_bias(rel_bias)
    cos, sin = _rope_tables(seq)
    for l in range(depth):
        h = _mixer(h, norm_mix[l], w_in[l], q_gain[l], k_gain[l], ret_gain[l], w_out[l],
                   near_bias, cos, sin, batch, seq)
        j = l // 2
        if l % 2 == 0:
            hid = _gate_up(h, norm_ffn[l], ffn_w_gate[j].astype(MXU_DTYPE),
                           ffn_w_up[j].astype(MXU_DTYPE))
            f = ffn_w_down.shape[1]
            h = _down(hid, ffn_w_down[j].astype(MXU_DTYPE), h)
        else:
            h = _moe(h, norm_ffn[l], moe_router[j], moe_w_gate[j], moe_w_up[j], moe_w_down[j])
    return h.reshape(batch, seq, d)
```

```python
import functools
import math

import jax
import jax.numpy as jnp
import numpy as np
from jax import lax
from jax.experimental import pallas as pl
from jax.experimental.pallas import tpu as pltpu

CHUNK = 64
A_HEADS = 8
A_KV_HEADS = 2
A_HEAD_DIM = 128
IDX_HEADS = 16
IDX_DIM = 64
TOPK_MAX = 256
REL_BUCKETS = 32
REL_MAX_DIST = 128
R_HEADS = 4
R_HEAD_DIM = 256
ROPE_BASE = 10000.0
A_Q = A_HEADS * A_HEAD_DIM
A_KV = A_KV_HEADS * A_HEAD_DIM
IDX_Q = IDX_HEADS * IDX_DIM
R_W = R_HEADS * R_HEAD_DIM
N_EXPERTS = 8
EPS = 1e-6
GN_EPS = 1e-5

LANES = 128
SUBLANES = 8
PACKED_ROWS = 16
HALF16 = 1 << 15
LOG2E = math.log2(math.e)
VMEM_LIMIT_BYTES = 56 * 1024 * 1024
MOE_VMEM_LIMIT_BYTES = 60 * 1024 * 1024
MXU_DTYPE = jnp.bfloat16
DSA_TQ = 256
RET_CHUNK = 256
MOE_TM = 512
COMBINE_TM = 256
ROUTE_LANE = 8
INT_MIN = -2147483648
MASKED = -1e30

_NT = (((1,), (1,)), ((), ()))
_TN = (((0,), (0,)), ((), ()))


def _cparams(*sem):
    return pltpu.CompilerParams(dimension_semantics=sem, vmem_limit_bytes=VMEM_LIMIT_BYTES)


def _dot(a, b):
    return jnp.dot(a, b, preferred_element_type=jnp.float32)


def _rms(x_ref, g_ref):
    x = x_ref[...]
    ms = jnp.mean(x * x, axis=-1, keepdims=True)
    return x * lax.rsqrt(ms + EPS) * g_ref[...]


def _norm_router_kernel(x_ref, g_ref, wr_ref, o_ref, gates_ref):
    xn = _rms(x_ref, g_ref)
    o_ref[...] = xn.astype(o_ref.dtype)
    x_hi = xn.astype(jnp.bfloat16)
    x_lo = (xn - x_hi.astype(jnp.float32)).astype(jnp.bfloat16)
    w = wr_ref[...]
    w_hi = w.astype(jnp.bfloat16)
    w_lo = (w - w_hi.astype(jnp.float32)).astype(jnp.bfloat16)
    logits = _dot(x_hi, w_hi) + (_dot(x_lo, w_hi) + _dot(x_hi, w_lo))
    lane = lax.broadcasted_iota(jnp.int32, logits.shape, 1)
    neg = jnp.float32(-jnp.inf)
    logits = jnp.where(lane < N_EXPERTS, logits, neg)
    m1 = jnp.max(logits, axis=-1, keepdims=True)
    i1 = jnp.min(jnp.where(logits == m1, lane, LANES), axis=-1, keepdims=True)
    rest = jnp.where(lane == i1, neg, logits)
    m2 = jnp.max(rest, axis=-1, keepdims=True)
    i2 = jnp.min(jnp.where(rest == m2, lane, LANES), axis=-1, keepdims=True)
    e2 = jnp.exp(m2 - m1)
    g1 = 1.0 / (1.0 + e2)
    g2 = e2 / (1.0 + e2)
    route = jnp.where(lane == ROUTE_LANE, g1, 0.0) + jnp.where(lane == ROUTE_LANE + 1, g2, 0.0)
    route = route + jnp.where(lane == ROUTE_LANE + 2, i1.astype(jnp.float32), 0.0)
    route = route + jnp.where(lane == ROUTE_LANE + 3, i2.astype(jnp.float32), 0.0)
    gates_ref[...] = route


def _rmsnorm_router(x, gain, router_w, tm=512):
    n, d = x.shape
    wr = jnp.pad(router_w, ((0, 0), (0, LANES - router_w.shape[1])))
    return pl.pallas_call(
        _norm_router_kernel, grid=(n // tm,),
        in_specs=[pl.BlockSpec((tm, d), lambda i: (i, 0)),
                  pl.BlockSpec((1, d), lambda i: (0, 0)),
                  pl.BlockSpec((d, LANES), lambda i: (0, 0))],
        out_specs=[pl.BlockSpec((tm, d), lambda i: (i, 0)),
                   pl.BlockSpec((tm, LANES), lambda i: (i, 0))],
        out_shape=[jax.ShapeDtypeStruct((n, d), jnp.float32),
                   jax.ShapeDtypeStruct((n, LANES), jnp.float32)],
        compiler_params=_cparams("parallel"), name="rmsnorm_router")(x, gain.reshape(1, d), wr)


PROJ_TN = 1024
COL_RQK = 0
COL_RVG = COL_RQK + 2 * R_W
COL_Q = COL_RVG + 2 * R_W
COL_IQ = COL_Q + A_Q
COL_KV = COL_IQ + IDX_Q
COL_IKW = COL_KV + 2 * A_KV
IKW_PAD = PROJ_TN - 2 * A_KV
PROJ_WIDTH = COL_IKW + IKW_PAD
assert COL_Q % PROJ_TN == 0 and COL_KV % PROJ_TN == 0 and PROJ_WIDTH % PROJ_TN == 0


def _proj_kernel(x_ref, g_ref, w_ref, hg_ref, o_ref, ikw_ref, xn_ref):
    j = pl.program_id(1)

    @pl.when(j == 0)
    def _():
        xn_ref[...] = _rms(x_ref, g_ref).astype(xn_ref.dtype)

    acc = _dot(xn_ref[...], w_ref[...])
    o_ref[...] = acc.astype(o_ref.dtype)

    def renorm(n_heads, gain_row):
        for h in range(n_heads):
            y = acc[:, h * A_HEAD_DIM:(h + 1) * A_HEAD_DIM]
            ms = jnp.mean(y * y, axis=-1, keepdims=True)
            y = y * lax.rsqrt(ms + EPS) * hg_ref[gain_row:gain_row + 1, :]
            o_ref[:, h * A_HEAD_DIM:(h + 1) * A_HEAD_DIM] = y.astype(o_ref.dtype)

    @pl.when(jnp.logical_and(j >= COL_Q // PROJ_TN, j < COL_IQ // PROJ_TN))
    def _():
        renorm(PROJ_TN // A_HEAD_DIM, 0)

    @pl.when(j == COL_KV // PROJ_TN)
    def _():
        renorm(A_KV // A_HEAD_DIM, 1)
        off = COL_IKW - COL_KV
        ikw_ref[...] = acc[:, off:off + LANES]


def _project(x, norm_gain, w_all, head_gains, tm=1024):
    n, d = x.shape
    return pl.pallas_call(
        _proj_kernel, grid=(n // tm, PROJ_WIDTH // PROJ_TN),
        in_specs=[pl.BlockSpec((tm, d), lambda i, j: (i, 0)),
                  pl.BlockSpec((1, d), lambda i, j: (0, 0)),
                  pl.BlockSpec((d, PROJ_TN), lambda i, j: (0, j)),
                  pl.BlockSpec((2, A_HEAD_DIM), lambda i, j: (0, 0))],
        out_specs=[pl.BlockSpec((tm, PROJ_TN), lambda i, j: (i, j)),
                   pl.BlockSpec((tm, LANES), lambda i, j: (i, 0))],
        out_shape=[jax.ShapeDtypeStruct((n, PROJ_WIDTH), MXU_DTYPE),
                   jax.ShapeDtypeStruct((n, LANES), jnp.float32)],
        scratch_shapes=[pltpu.VMEM((tm, d), MXU_DTYPE)],
        compiler_params=_cparams("parallel", "arbitrary"), name="in_proj")(
            x, norm_gain.reshape(1, d), w_all, head_gains)


def _merged_in_weights(w_in):
    c0 = A_Q
    c1 = c0 + 2 * A_KV
    c2 = c1 + IDX_Q
    c3 = c2 + IDX_DIM + IDX_HEADS
    c4 = c3 + 2 * R_W
    ikw = jnp.pad(w_in[:, c2:c3], ((0, 0), (0, IKW_PAD - (c3 - c2))))
    parts = [w_in[:, c3:c4], w_in[:, c4:], w_in[:, :c0], w_in[:, c1:c2], w_in[:, c0:c1], ikw]
    return jnp.concatenate(parts, axis=1).astype(MXU_DTYPE)


def _outproj_kernel(a_ref, r_ref, wa_ref, wr_ref, res_ref, o_ref):
    o_ref[...] = res_ref[...] + (_dot(a_ref[...], wa_ref[...]) + _dot(r_ref[...], wr_ref[...]))


def _out_project(a, r, w, res, tm=512, tn=2048):
    n, ka = a.shape
    kr = r.shape[1]
    assert ka == kr
    d = w.shape[1]
    tn = min(tn, d)
    return pl.pallas_call(
        _outproj_kernel, grid=(n // tm, d // tn),
        in_specs=[pl.BlockSpec((tm, ka), lambda i, j: (i, 0)),
                  pl.BlockSpec((tm, kr), lambda i, j: (i, 0)),
                  pl.BlockSpec((ka, tn), lambda i, j: (0, j)),
                  pl.BlockSpec((kr, tn), lambda i, j: (1, j)),
                  pl.BlockSpec((tm, tn), lambda i, j: (i, j))],
        out_specs=pl.BlockSpec((tm, tn), lambda i, j: (i, j)),
        out_shape=jax.ShapeDtypeStruct((n, d), jnp.float32),
        compiler_params=_cparams("parallel", "parallel"), name="out_proj")(a, r, w, w, res)


def _swiglu_hidden(x, wg, wu):
    g = _dot(x, wg)
    u = _dot(x, wu)
    return g * jax.nn.sigmoid(g) * u


def _gu_kernel(x_ref, g_ref, wg_ref, wu_ref, o_ref, xn_ref):
    @pl.when(pl.program_id(1) == 0)
    def _():
        xn_ref[...] = _rms(x_ref, g_ref).astype(xn_ref.dtype)

    o_ref[...] = _swiglu_hidden(xn_ref[...], wg_ref[...], wu_ref[...]).astype(o_ref.dtype)


def _gate_up(x, norm_gain, wg, wu, tm=1024, tn=512):
    n, d = x.shape
    f = wg.shape[1]
    return pl.pallas_call(
        _gu_kernel, grid=(n // tm, f // tn),
        in_specs=[pl.BlockSpec((tm, d), lambda i, j: (i, 0)),
                  pl.BlockSpec((1, d), lambda i, j: (0, 0)),
                  pl.BlockSpec((d, tn), lambda i, j: (0, j)),
                  pl.BlockSpec((d, tn), lambda i, j: (0, j))],
        out_specs=pl.BlockSpec((tm, tn), lambda i, j: (i, j)),
        out_shape=jax.ShapeDtypeStruct((n, f), MXU_DTYPE),
        scratch_shapes=[pltpu.VMEM((tm, d), MXU_DTYPE)],
        compiler_params=_cparams("parallel", "arbitrary"), name="gate_up")(
            x, norm_gain.reshape(1, d), wg, wu)


def _down_kernel(x_ref, w_ref, res_ref, o_ref):
    o_ref[...] = res_ref[...] + _dot(x_ref[...], w_ref[...])


def _down(x, w, res, tm=512, tn=1024):
    n, kt = x.shape
    d = w.shape[1]
    tn = min(tn, d)
    return pl.pallas_call(
        _down_kernel, grid=(d // tn, n // tm),
        in_specs=[pl.BlockSpec((tm, kt), lambda j, i: (i, 0)),
                  pl.BlockSpec((kt, tn), lambda j, i: (0, j)),
                  pl.BlockSpec((tm, tn), lambda j, i: (i, j))],
        out_specs=pl.BlockSpec((tm, tn), lambda j, i: (i, j)),
        out_shape=jax.ShapeDtypeStruct((n, d), jnp.float32),
        compiler_params=_cparams("parallel", "parallel"), name="down_proj")(x, w, res)


def _route_tables(route, tm):
    n = route.shape[0]
    e = route[:, ROUTE_LANE + 2:ROUTE_LANE + 4].astype(jnp.int32).reshape(-1)
    onehot = (e[:, None] == jnp.arange(N_EXPERTS, dtype=jnp.int32)[None, :]).astype(jnp.int32)
    csum = jnp.cumsum(onehot, axis=0)
    rank = jnp.sum(csum * onehot, axis=1) - 1
    tiles_per = (csum[-1] + tm - 1) // tm
    tile_end = jnp.cumsum(tiles_per)
    row_start = (tile_end - tiles_per) * tm
    pos = jnp.sum(onehot * row_start[None, :], axis=1) + rank
    n_tiles = (2 * n) // tm + N_EXPERTS
    t = jnp.arange(n_tiles, dtype=jnp.int32)
    tile_expert = jnp.sum((t[:, None] >= tile_end[None, :]).astype(jnp.int32), axis=1)
    meta = jnp.concatenate([jnp.minimum(tile_expert, N_EXPERTS - 1), tile_end[-1:]]).astype(jnp.int32)
    src = jnp.zeros((n_tiles * tm,), jnp.int32).at[pos].set(
        jnp.arange(2 * n, dtype=jnp.int32) // 2, unique_indices=True)
    return src.reshape(n_tiles, 1, tm), pos.reshape(n, 2), meta


def _row_copy(src_hbm, row, dst, slot, r, sem):
    return pltpu.make_async_copy(src_hbm.at[pl.ds(row, 1)], dst.at[slot, pl.ds(r, 1)], sem)


def _gather_kernel(idx_ref, nxt_ref, x_hbm, o_ref, buf, sem):
    i = pl.program_id(0)
    tm = buf.shape[1]
    slot = lax.rem(i, 2)

    def issue(ref, s):
        def body(k, c):
            for u in range(2):
                r = 2 * k + u
                _row_copy(x_hbm, ref[0, 0, r], buf, s, r, sem.at[s]).start(priority=u)
            return c
        lax.fori_loop(0, tm // 2, body, 0, unroll=4)

    @pl.when(i == 0)
    def _():
        issue(idx_ref, 0)

    @pl.when(i + 1 < pl.num_programs(0))
    def _():
        issue(nxt_ref, 1 - slot)

    pltpu.make_async_copy(x_hbm.at[pl.ds(0, tm)], buf.at[slot], sem.at[slot]).wait()
    o_ref[...] = buf[slot].astype(o_ref.dtype)


def _moe_gather(x, src):
    n_tiles, _, tm = src.shape
    d = x.shape[1]
    idx_spec = pl.BlockSpec((1, 1, tm), lambda i: (i, 0, 0), memory_space=pltpu.SMEM)
    nxt_spec = pl.BlockSpec((1, 1, tm), lambda i: (jnp.minimum(i + 1, n_tiles - 1), 0, 0),
                            memory_space=pltpu.SMEM)
    return pl.pallas_call(
        _gather_kernel, grid=(n_tiles,),
        in_specs=[idx_spec, nxt_spec, pl.BlockSpec(memory_space=pl.ANY)],
        out_specs=pl.BlockSpec((tm, d), lambda i: (i, 0)),
        out_shape=jax.ShapeDtypeStruct((n_tiles * tm, d), MXU_DTYPE),
        scratch_shapes=[pltpu.VMEM((2, tm, d), x.dtype), pltpu.SemaphoreType.DMA((2,))],
        compiler_params=_cparams("arbitrary"), name="moe_gather")(src, src, x)


def _moe_gu_kernel(meta_ref, x_ref, wg_ref, wu_ref, o_ref, wgb_ref, wub_ref):
    i = pl.program_id(1)
    used = i < meta_ref[pl.num_programs(1)]
    new_block = jnp.logical_or(i == 0, meta_ref[i] != meta_ref[jnp.maximum(i - 1, 0)])

    @pl.when(jnp.logical_and(used, new_block))
    def _():
        wgb_ref[...] = wg_ref[0].astype(wgb_ref.dtype)
        wub_ref[...] = wu_ref[0].astype(wub_ref.dtype)

    @pl.when(used)
    def _():
        o_ref[...] = _swiglu_hidden(x_ref[...], wgb_ref[...], wub_ref[...]).astype(o_ref.dtype)

    @pl.when(jnp.logical_not(used))
    def _():
        o_ref[...] = jnp.zeros_like(o_ref)


def _moe_gate_up(xs, wg, wu, meta, tm, tn=1024):
    p, d = xs.shape
    f = wg.shape[2]
    tn = min(tn, f)
    grid_spec = pltpu.PrefetchScalarGridSpec(
        num_scalar_prefetch=1, grid=(f // tn, p // tm),
        in_specs=[pl.BlockSpec((tm, d), lambda j, i, m: (i, 0)),
                  pl.BlockSpec((1, d, tn), lambda j, i, m: (m[i], 0, j)),
                  pl.BlockSpec((1, d, tn), lambda j, i, m: (m[i], 0, j))],
        out_specs=pl.BlockSpec((tm, tn), lambda j, i, m: (i, j)),
        scratch_shapes=[pltpu.VMEM((d, tn), MXU_DTYPE), pltpu.VMEM((d, tn), MXU_DTYPE)])
    return pl.pallas_call(
        _moe_gu_kernel, grid_spec=grid_spec,
        out_shape=jax.ShapeDtypeStruct((p, f), MXU_DTYPE),
        compiler_params=pltpu.CompilerParams(dimension_semantics=("arbitrary", "arbitrary"),
                                             vmem_limit_bytes=MOE_VMEM_LIMIT_BYTES),
        name="moe_gate_up")(meta, xs, wg, wu)


def _moe_down_kernel(meta_ref, x_ref, w_ref, o_ref, wb_ref):
    i = pl.program_id(1)
    used = i < meta_ref[pl.num_programs(1)]
    new_block = jnp.logical_or(i == 0, meta_ref[i] != meta_ref[jnp.maximum(i - 1, 0)])

    @pl.when(jnp.logical_and(used, new_block))
    def _():
        wb_ref[...] = w_ref[0].astype(wb_ref.dtype)

    @pl.when(used)
    def _():
        o_ref[...] = _dot(x_ref[...], wb_ref[...])

    @pl.when(jnp.logical_not(used))
    def _():
        o_ref[...] = jnp.zeros_like(o_ref)


def _moe_down(hid, wd, meta, tm, tn=512):
    p, f = hid.shape
    d = wd.shape[2]
    tn = min(tn, d)
    grid_spec = pltpu.PrefetchScalarGridSpec(
        num_scalar_prefetch=1, grid=(d // tn, p // tm),
        in_specs=[pl.BlockSpec((tm, f), lambda j, i, m: (i, 0)),
                  pl.BlockSpec((1, f, tn), lambda j, i, m: (m[i], 0, j))],
        out_specs=pl.BlockSpec((tm, tn), lambda j, i, m: (i, j)),
        scratch_shapes=[pltpu.VMEM((f, tn), MXU_DTYPE)])
    return pl.pallas_call(
        _moe_down_kernel, grid_spec=grid_spec,
        out_shape=jax.ShapeDtypeStruct((p, d), jnp.float32),
        compiler_params=pltpu.CompilerParams(dimension_semantics=("arbitrary", "arbitrary"),
                                             vmem_limit_bytes=MOE_VMEM_LIMIT_BYTES),
        name="moe_down")(meta, hid, wd)


def _combine_kernel(pa_ref, pb_ref, pa_nxt, pb_nxt, y_hbm, h_ref, route_ref, o_ref, bufa, bufb, sem):
    i = pl.program_id(0)
    tm = bufa.shape[1]
    slot = lax.rem(i, 2)

    def issue(pa, pb, s):
        def body(r, c):
            _row_copy(y_hbm, pa[0, 0, r], bufa, s, r, sem.at[0, s]).start(priority=0)
            _row_copy(y_hbm, pb[0, 0, r], bufb, s, r, sem.at[1, s]).start(priority=1)
            return c
        lax.fori_loop(0, tm, body, 0, unroll=4)

    @pl.when(i == 0)
    def _():
        issue(pa_ref, pb_ref, 0)

    @pl.when(i + 1 < pl.num_programs(0))
    def _():
        issue(pa_nxt, pb_nxt, 1 - slot)

    pltpu.make_async_copy(y_hbm.at[pl.ds(0, tm)], bufa.at[slot], sem.at[0, slot]).wait()
    pltpu.make_async_copy(y_hbm.at[pl.ds(0, tm)], bufb.at[slot], sem.at[1, slot]).wait()
    route = route_ref[...]
    ga = route[:, ROUTE_LANE:ROUTE_LANE + 1]
    gb = route[:, ROUTE_LANE + 1:ROUTE_LANE + 2]
    o_ref[...] = h_ref[...] + (ga * bufa[slot] + gb * bufb[slot])


def _moe_combine(y, pos, h, route, tm):
    n, d = h.shape
    nt = n // tm
    pa = pos[:, 0].reshape(nt, 1, tm)
    pb = pos[:, 1].reshape(nt, 1, tm)
    cur = pl.BlockSpec((1, 1, tm), lambda i: (i, 0, 0), memory_space=pltpu.SMEM)
    nxt = pl.BlockSpec((1, 1, tm), lambda i: (jnp.minimum(i + 1, nt - 1), 0, 0), memory_space=pltpu.SMEM)
    return pl.pallas_call(
        _combine_kernel, grid=(nt,),
        in_specs=[cur, cur, nxt, nxt, pl.BlockSpec(memory_space=pl.ANY),
                  pl.BlockSpec((tm, d), lambda i: (i, 0)),
                  pl.BlockSpec((tm, LANES), lambda i: (i, 0))],
        out_specs=pl.BlockSpec((tm, d), lambda i: (i, 0)),
        out_shape=jax.ShapeDtypeStruct((n, d), jnp.float32),
        scratch_shapes=[pltpu.VMEM((2, tm, d), jnp.float32), pltpu.VMEM((2, tm, d), jnp.float32),
                        pltpu.SemaphoreType.DMA((2, 2))],
        compiler_params=_cparams("arbitrary"), name="moe_combine")(pa, pb, pa, pb, y, h, route)


def _moe(h, norm_gain, router_w, wg, wu, wd):
    hn, route = _rmsnorm_router(h, norm_gain, router_w)
    src, pos, meta = _route_tables(route, MOE_TM)
    xs = _moe_gather(hn, src)
    hid = _moe_gate_up(xs, wg, wu, meta, MOE_TM)
    y = _moe_down(hid, wd, meta, MOE_TM)
    return _moe_combine(y, pos, h, route, COMBINE_TM)


def _sortable(x):
    x = jnp.where(x == 0.0, 0.0, x)
    i = lax.bitcast_convert_type(x, jnp.int32)
    return i ^ ((i >> 31) & jnp.int32(0x7FFFFFFF))


def _dsa_kernel(q_ref, kv_ref, iq_ref, ikk_ref, ikq_ref, nb_ref, o_ref,
                iklo_ref, ikhi_ref, vt_ref, wt_ref, skey_ref, hi_ref, lo_ref, m_ref, l_ref, a_ref,
                acc_ref, s0_ref, s1_ref, p_ref, *, n_top):
    tq = DSA_TQ
    half_keys = tq // 2
    qi = pl.program_id(1)
    f32 = jnp.float32
    nkb = skey_ref.shape[0]

    @pl.when(qi == 0)
    def _():
        ik = ikk_ref[...]
        lane = lax.broadcasted_iota(jnp.int32, ik.shape, 1)
        iklo_ref[...] = jnp.where(lane < IDX_DIM, ik, 0.0).astype(MXU_DTYPE)
        shifted = pltpu.roll(ik, IDX_DIM, axis=1)
        ikhi_ref[...] = jnp.where(lane >= IDX_DIM, shifted, 0.0).astype(MXU_DTYPE)
        for kb in range(nkb):
            for g in range(A_KV_HEADS):
                v = kv_ref[kb * tq:(kb + 1) * tq, A_KV + g * A_HEAD_DIM:A_KV + (g + 1) * A_HEAD_DIM]
                vt_ref[g, kb] = v.astype(f32).T.astype(MXU_DTYPE)

    wt_ref[...] = ikq_ref[...].T

    key_iota = lax.broadcasted_iota(jnp.int32, (half_keys, tq), 0)
    t_chunk = (qi * tq + lax.broadcasted_iota(jnp.int32, (half_keys, tq), 1)) // CHUNK

    def score_block(kb, carry):
        for half in range(2):
            r = pl.multiple_of(kb * tq + half * half_keys, half_keys)
            klo = iklo_ref[pl.ds(r, half_keys), :]
            khi = ikhi_ref[pl.ds(r, half_keys), :]
            acc = jnp.zeros((half_keys, tq), f32)
            for p in range(IDX_HEADS // 2):
                rhs = iq_ref[:, p * LANES:(p + 1) * LANES]
                for lhs, h in ((klo, 2 * p), (khi, 2 * p + 1)):
                    hs = lax.dot_general(lhs, rhs, _NT, preferred_element_type=f32)
                    acc = acc + jnp.maximum(hs, 0.0) * wt_ref[IDX_DIM + h:IDX_DIM + h + 1, :]
            key = _sortable(acc)
            s_chunk = (r + key_iota) // CHUNK
            rows = slice(half * half_keys, (half + 1) * half_keys)
            key = jnp.where(s_chunk <= t_chunk, key, INT_MIN)
            skey_ref[kb, rows, :] = key
            hi_ref[kb, rows, :] = (key >> 16).astype(jnp.int16)
        return carry

    lax.fori_loop(0, qi + 1, score_block, 0)

    n_groups = tq // SUBLANES
    n_acc = 4
    sub_iota = lax.broadcasted_iota(jnp.int32, (SUBLANES, tq), 0)

    def rows8(x):
        return jnp.broadcast_to(x, (SUBLANES, tq))

    def count_where(pred):
        def body(kb, accs):
            accs = list(accs)
            for g in range(n_groups):
                keys = skey_ref[kb, g * SUBLANES:(g + 1) * SUBLANES, :]
                hit = jnp.where(pred(keys, kb * tq + g * SUBLANES), 1.0, 0.0)
                accs[g % n_acc] = accs[g % n_acc] + hit
            return tuple(accs)

        accs = lax.fori_loop(0, qi + 1, body,
                             tuple(jnp.zeros((SUBLANES, tq), f32) for _ in range(n_acc)))
        return jnp.sum((accs[0] + accs[1]) + (accs[2] + accs[3]), axis=0, keepdims=True)

    i16 = jnp.int16
    n_groups16 = tq // PACKED_ROWS

    def rows16(x):
        return jnp.broadcast_to(x, (PACKED_ROWS, tq))

    def count16(ref, pred):
        def body(kb, accs):
            accs = list(accs)
            for g in range(n_groups16):
                rows = ref[kb, g * PACKED_ROWS:(g + 1) * PACKED_ROWS, :]
                hit = jnp.where(pred(rows), i16(1), i16(0))
                accs[g % n_acc] = accs[g % n_acc] + hit
            return tuple(accs)

        accs = lax.fori_loop(0, qi + 1, body,
                             tuple(jnp.zeros((PACKED_ROWS, tq), i16) for _ in range(n_acc)))
        tot = (accs[0] + accs[1]) + (accs[2] + accs[3])
        return jnp.sum(tot.astype(jnp.int32), axis=0, keepdims=True)

    def search16(ref, target):
        def step(it, tu):
            bit = jnp.left_shift(jnp.int32(1), 15 - it)
            cand = rows16(((tu | bit) - HALF16).astype(i16))
            cnt = count16(ref, lambda rows: rows >= cand)
            return jnp.where(cnt >= target, tu | bit, tu)

        return lax.fori_loop(0, 16, step, jnp.zeros((1, tq), jnp.int32))

    t_hi = search16(hi_ref, n_top) - HALF16
    t_hi16 = rows16(t_hi.astype(i16))
    above = count16(hi_ref, lambda rows: rows > t_hi16)
    t_hi8 = rows8(t_hi)

    def low_block(kb, carry):
        for g in range(n_groups16):
            rows = slice(g * PACKED_ROWS, (g + 1) * PACKED_ROWS)
            keys = skey_ref[kb, rows, :]
            t2 = jnp.concatenate([t_hi8, t_hi8], axis=0)
            low = jnp.where((keys >> 16) == t2, (keys & 0xFFFF) - HALF16, -HALF16)
            lo_ref[kb, rows, :] = low.astype(i16)
        return carry

    lax.fori_loop(0, qi + 1, low_block, 0)
    t_lo = search16(lo_ref, n_top - above)
    thr = rows8(t_hi * (2 * HALF16) + t_lo)

    t_lo16 = rows16((t_lo - HALF16).astype(i16))
    n_gt = above + count16(lo_ref, lambda rows: rows > t_lo16)
    n_eq = count16(lo_ref, lambda rows: rows == t_lo16).astype(f32)
    need = (n_top - n_gt).astype(f32)
    n_bits = max(1, int(math.ceil(math.log2(nkb * tq + 1))))

    def tie_search():
        def jstep(it, jb):
            bit = jnp.left_shift(jnp.int32(1), n_bits - 1 - it)
            bound = rows8(jb | bit)
            cnt = count_where(lambda keys, k0: (keys == thr) & ((k0 + sub_iota) < bound))
            return jnp.where(cnt <= need, jb | bit, jb)

        return lax.fori_loop(0, n_bits, jstep, jnp.zeros((1, tq), jnp.int32))

    jb = lax.cond(jnp.max(n_eq - need) > 0.0, tie_search,
                  lambda: jnp.full((1, tq), 2 ** 30, jnp.int32))
    bound = rows8(jb)

    def mask_block(kb, carry):
        for g in range(n_groups):
            rows = slice(g * SUBLANES, (g + 1) * SUBLANES)
            keys = skey_ref[kb, rows, :]
            tie = (keys == thr) & ((kb * tq + g * SUBLANES + sub_iota) < bound)
            sel = ((keys > thr) | tie) & (keys != INT_MIN)
            skey_ref[kb, rows, :] = lax.bitcast_convert_type(
                jnp.where(sel, 0.0, MASKED).astype(f32), jnp.int32)
        return carry

    lax.fori_loop(0, qi + 1, mask_block, 0)

    m_ref[...] = jnp.full(m_ref.shape, MASKED, f32)
    l_ref[...] = jnp.zeros_like(l_ref)
    acc_ref[...] = jnp.zeros_like(acc_ref)
    grp = A_HEADS // A_KV_HEADS

    def logits(kb_raw, s_ref):
        kb = jnp.minimum(kb_raw, qi)
        r = pl.multiple_of(kb * tq, tq)
        nb_row = pl.multiple_of(jnp.clip(kb - (qi - 2), 0, 2) * tq, tq)
        beyond = jnp.where(kb_raw > qi, MASKED, 0.0).astype(f32)
        mb = lax.bitcast_convert_type(skey_ref[kb], f32) + beyond
        for h in range(A_HEADS):
            g = h // grp
            kblk = kv_ref[pl.ds(r, tq), g * A_HEAD_DIM:(g + 1) * A_HEAD_DIM]
            qh = q_ref[:, h * A_HEAD_DIM:(h + 1) * A_HEAD_DIM]
            s = lax.dot_general(kblk, qh, _NT, preferred_element_type=f32)
            s_ref[h] = s + (mb + nb_ref[h, pl.ds(nb_row, tq), :])

    def softmax_pv(kb_raw, s_ref):
        kb = jnp.minimum(kb_raw, qi)
        for h in range(A_HEADS):
            m_prev = m_ref[h]
            m_new = jnp.maximum(m_prev, jnp.max(s_ref[h], axis=0, keepdims=True))
            alpha = jnp.exp2(m_prev - m_new)
            p = jnp.exp2(s_ref[h] - m_new)
            l_ref[h] = alpha * l_ref[h] + jnp.sum(p, axis=0, keepdims=True)
            m_ref[h] = m_new
            a_ref[h] = alpha
            p_ref[h] = p.astype(MXU_DTYPE)
        for h in range(A_HEADS):
            vt = vt_ref[h // grp, kb]
            acc_ref[h] = a_ref[h] * acc_ref[h] + _dot(vt, p_ref[h])

    def attend_pair(j, carry):
        logits(2 * j + 1, s1_ref)
        softmax_pv(2 * j, s0_ref)
        logits(2 * j + 2, s0_ref)
        softmax_pv(2 * j + 1, s1_ref)
        return carry

    logits(0, s0_ref)
    lax.fori_loop(0, (qi + 2) // 2, attend_pair, 0)

    for h in range(A_HEADS):
        o = (acc_ref[h] / l_ref[h]).T
        o_ref[:, h * A_HEAD_DIM:(h + 1) * A_HEAD_DIM] = o.astype(o_ref.dtype)


def _t5_bucket(rel):
    half = REL_BUCKETS // 2
    max_exact = half // 2
    ret = jnp.where(rel > 0, half, 0)
    n = jnp.abs(rel)
    nf = jnp.maximum(n, 1).astype(jnp.float32)
    large = max_exact + (jnp.log(nf / max_exact) / math.log(REL_MAX_DIST / max_exact)
                         * (half - max_exact)).astype(jnp.int32)
    large = jnp.minimum(large, half - 1)
    return ret + jnp.where(n < max_exact, n, large)


def _near_bias(rel_bias):
    tq = DSA_TQ
    c = jnp.arange(2 * tq, dtype=jnp.int32)[:, None]
    i = jnp.arange(tq, dtype=jnp.int32)[None, :]
    rel = c - tq - i
    far = rel_bias[_t5_bucket(jnp.full((1,), -REL_MAX_DIST, jnp.int32))]
    onehot = (_t5_bucket(rel)[None] == jnp.arange(REL_BUCKETS, dtype=jnp.int32)[:, None, None])
    table = (rel_bias - far).T * LOG2E
    near = jnp.einsum('hb,bci->hci', table, onehot.astype(jnp.float32),
                      precision=lax.Precision.HIGHEST)
    return jnp.pad(near, ((0, 0), (tq, 0), (0, 0)))


def _dsa(proj, ikw, near_bias, batch, seq):
    tq = DSA_TQ
    nq = seq // tq
    n_top = min(TOPK_MAX, seq // 4)
    return pl.pallas_call(
        functools.partial(_dsa_kernel, n_top=n_top),
        grid=(batch, nq),
        in_specs=[pl.BlockSpec((tq, A_Q), lambda b, i: (b * nq + i, COL_Q // A_Q)),
                  pl.BlockSpec((seq, 2 * A_KV), lambda b, i: (b, COL_KV // (2 * A_KV))),
                  pl.BlockSpec((tq, IDX_Q), lambda b, i: (b * nq + i, COL_IQ // IDX_Q)),
                  pl.BlockSpec((seq, LANES), lambda b, i: (b, 0)),
                  pl.BlockSpec((tq, LANES), lambda b, i: (b * nq + i, 0)),
                  pl.BlockSpec((A_HEADS, 3 * tq, tq), lambda b, i: (0, 0, 0),
                               pipeline_mode=pl.Buffered(1))],
        out_specs=pl.BlockSpec((tq, A_Q), lambda b, i: (b * nq + i, 0)),
        out_shape=jax.ShapeDtypeStruct((batch * seq, A_Q), MXU_DTYPE),
        scratch_shapes=[pltpu.VMEM((seq, LANES), MXU_DTYPE),
                        pltpu.VMEM((seq, LANES), MXU_DTYPE),
                        pltpu.VMEM((A_KV_HEADS, nq, A_HEAD_DIM, tq), MXU_DTYPE),
                        pltpu.VMEM((LANES, tq), jnp.float32),
                        pltpu.VMEM((nq, tq, tq), jnp.int32),
                        pltpu.VMEM((nq, tq, tq), jnp.int16),
                        pltpu.VMEM((nq, tq, tq), jnp.int16),
                        pltpu.VMEM((A_HEADS, 1, tq), jnp.float32),
                        pltpu.VMEM((A_HEADS, 1, tq), jnp.float32),
                        pltpu.VMEM((A_HEADS, 1, tq), jnp.float32),
                        pltpu.VMEM((A_HEADS, A_HEAD_DIM, tq), jnp.float32),
                        pltpu.VMEM((A_HEADS, tq, tq), jnp.float32),
                        pltpu.VMEM((A_HEADS, tq, tq), jnp.float32),
                        pltpu.VMEM((A_HEADS, tq, tq), MXU_DTYPE)],
        compiler_params=_cparams("arbitrary", "arbitrary"), name="dsa")(
            proj, proj, proj, ikw, ikw, near_bias)


def _ret_kernel(qk_ref, vg_ref, cos_ref, sin_ref, gn_ref, o_ref, state_ref, dm_ref, dz_ref):
    c_len = RET_CHUNK
    f32 = jnp.float32
    hd = R_HEAD_DIM
    half = hd // 2
    log_g = [math.log(1.0 - 2.0 ** (-5.0 - h)) for h in range(R_HEADS)]

    @pl.when(pl.program_id(1) == 0)
    def _():
        state_ref[...] = jnp.zeros_like(state_ref)
        i = lax.broadcasted_iota(jnp.int32, (c_len, c_len), 0)
        j = lax.broadcasted_iota(jnp.int32, (c_len, c_len), 1)
        diff = (i - j).astype(f32)
        row = lax.broadcasted_iota(jnp.int32, (c_len, half), 0).astype(f32)
        for h in range(R_HEADS):
            dm_ref[h] = jnp.where(diff >= 0, jnp.exp(log_g[h] * jnp.maximum(diff, 0.0)), 0.0)
            dz_ref[h, 0] = jnp.exp(log_g[h] * (row + 1.0))
            dz_ref[h, 1] = jnp.exp(log_g[h] * (c_len - 1.0 - row))

    cos = cos_ref[...]
    sin = sin_ref[...]

    def rot(x):
        x1, x2 = x[:, :half], x[:, half:]
        return jnp.concatenate([x1 * cos - x2 * sin, x1 * sin + x2 * cos], axis=1)

    for h in range(R_HEADS):
        cols = slice(h * hd, (h + 1) * hd)
        qf = rot(qk_ref[:, cols].astype(f32))
        kf = rot(qk_ref[:, R_W + h * hd:R_W + (h + 1) * hd].astype(f32)) * (hd ** -0.5)
        v = vg_ref[:, cols]
        att = lax.dot_general(qf.astype(MXU_DTYPE), kf.astype(MXU_DTYPE), _NT,
                              preferred_element_type=f32) * dm_ref[h]
        xi = jnp.concatenate([dz_ref[h, 0]] * 2, axis=1)
        zeta = jnp.concatenate([dz_ref[h, 1]] * 2, axis=1)
        state = state_ref[h]
        o = _dot(att.astype(MXU_DTYPE), v) + _dot((qf * xi).astype(MXU_DTYPE), state.astype(MXU_DTYPE))
        state_ref[h] = state * math.exp(log_g[h] * c_len) + lax.dot_general(
            (kf * zeta).astype(MXU_DTYPE), v, _TN, preferred_element_type=f32)
        mu = jnp.mean(o, axis=-1, keepdims=True)
        d = o - mu
        var = jnp.mean(d * d, axis=-1, keepdims=True)
        y = d * lax.rsqrt(var + GN_EPS) * gn_ref[:, cols]
        g = vg_ref[:, R_W + h * hd:R_W + (h + 1) * hd].astype(f32)
        o_ref[:, cols] = (g * jax.nn.sigmoid(g) * y).astype(o_ref.dtype)


def _rope_tables(seq):
    half = R_HEAD_DIM // 2
    inv = ROPE_BASE ** (-jnp.arange(half, dtype=jnp.float32) / half)
    ang = jnp.arange(seq, dtype=jnp.float32)[:, None] * inv[None, :]
    return jnp.cos(ang), jnp.sin(ang)


def _retention(proj, cos, sin, gn_gain, batch, seq):
    c_len = RET_CHUNK
    nc = seq // c_len
    half = R_HEAD_DIM // 2
    return pl.pallas_call(
        _ret_kernel, grid=(batch, nc),
        in_specs=[pl.BlockSpec((c_len, 2 * R_W), lambda b, c: (b * nc + c, COL_RQK // (2 * R_W))),
                  pl.BlockSpec((c_len, 2 * R_W), lambda b, c: (b * nc + c, COL_RVG // (2 * R_W))),
                  pl.BlockSpec((c_len, half), lambda b, c: (c, 0)),
                  pl.BlockSpec((c_len, half), lambda b, c: (c, 0)),
                  pl.BlockSpec((1, R_W), lambda b, c: (0, 0))],
        out_specs=pl.BlockSpec((c_len, R_W), lambda b, c: (b * nc + c, 0)),
        out_shape=jax.ShapeDtypeStruct((batch * seq, R_W), MXU_DTYPE),
        scratch_shapes=[pltpu.VMEM((R_HEADS, R_HEAD_DIM, R_HEAD_DIM), jnp.float32),
                        pltpu.VMEM((R_HEADS, c_len, c_len), jnp.float32),
                        pltpu.VMEM((R_HEADS, 2, c_len, half), jnp.float32)],
        compiler_params=_cparams("arbitrary", "arbitrary"), name="retention")(
            proj, proj, cos, sin, gn_gain.reshape(1, R_W))


def _mixer(h, norm_gain, w_in, q_gain, k_gain, gn_gain, w_out, near_bias, cos, sin, batch, seq):
    head_gains = jnp.stack([q_gain * (A_HEAD_DIM ** -0.5 * LOG2E), k_gain])
    proj, ikw = _project(h, norm_gain, _merged_in_weights(w_in), head_gains)
    a_out = _dsa(proj, ikw, near_bias, batch, seq)
    r_out = _retention(proj, cos, sin, gn_gain, batch, seq)
    return _out_project(a_out, r_out, w_out.astype(MXU_DTYPE), h)


def kernel(x, rel_bias, norm_mix, w_in, q_gain, k_gain, ret_gain, w_out, norm_ffn,
           ffn_w_gate, ffn_w_up, ffn_w_down, moe_router, moe_w_gate, moe_w_up, moe_w_down):
    batch, seq, d = x.shape
    depth = norm_mix.shape[0]
    h = x.reshape(batch * seq, d)
    near_bias = _near_bias(rel_bias)
    cos, sin = _rope_tables(seq)
    for l in range(depth):
        h = _mixer(h, norm_mix[l], w_in[l], q_gain[l], k_gain[l], ret_gain[l], w_out[l],
                   near_bias, cos, sin, batch, seq)
        j = l // 2
        if l % 2 == 0:
            hid = _gate_up(h, norm_ffn[l], ffn_w_gate[j].astype(MXU_DTYPE),
                           ffn_w_up[j].astype(MXU_DTYPE))
            f = ffn_w_down.shape[1]
            h = _down(hid, ffn_w_down[j].astype(MXU_DTYPE), h)
        else:
            h = _moe(h, norm_ffn[l], moe_router[j], moe_w_gate[j], moe_w_up[j], moe_w_down[j])
    return h.reshape(batch, seq, d)
```

```python
import functools
import math

import jax
import jax.numpy as jnp
import numpy as np
from jax import lax
from jax.experimental import pallas as pl
from jax.experimental.pallas import tpu as pltpu

CHUNK = 64
A_HEADS = 8
A_KV_HEADS = 2
A_HEAD_DIM = 128
IDX_HEADS = 16
IDX_DIM = 64
TOPK_MAX = 256
REL_BUCKETS = 32
REL_MAX_DIST = 128
R_HEADS = 4
R_HEAD_DIM = 256
ROPE_BASE = 10000.0
A_Q = A_HEADS * A_HEAD_DIM
A_KV = A_KV_HEADS * A_HEAD_DIM
IDX_Q = IDX_HEADS * IDX_DIM
R_W = R_HEADS * R_HEAD_DIM
N_EXPERTS = 8
EPS = 1e-6
GN_EPS = 1e-5

LANES = 128
SUBLANES = 8
PACKED_ROWS = 16
HALF16 = 1 << 15
LOG2E = math.log2(math.e)
VMEM_LIMIT_BYTES = 56 * 1024 * 1024
MOE_VMEM_LIMIT_BYTES = 60 * 1024 * 1024
MXU_DTYPE = jnp.bfloat16
DSA_TQ = 256
RET_CHUNK = 256
MOE_TM = 512
COMBINE_TM = 256
ROUTE_LANE = 8
INT_MIN = -2147483648
MASKED = -1e30

_NT = (((1,), (1,)), ((), ()))
_TN = (((0,), (0,)), ((), ()))


def _cparams(*sem):
    return pltpu.CompilerParams(dimension_semantics=sem, vmem_limit_bytes=VMEM_LIMIT_BYTES)


def _dot(a, b):
    return jnp.dot(a, b, preferred_element_type=jnp.float32)


def _rms(x_ref, g_ref):
    x = x_ref[...]
    ms = jnp.mean(x * x, axis=-1, keepdims=True)
    return x * lax.rsqrt(ms + EPS) * g_ref[...]


def _norm_router_kernel(x_ref, g_ref, wr_ref, o_ref, gates_ref):
    xn = _rms(x_ref, g_ref)
    o_ref[...] = xn.astype(o_ref.dtype)
    x_hi = xn.astype(jnp.bfloat16)
    x_lo = (xn - x_hi.astype(jnp.float32)).astype(jnp.bfloat16)
    w = wr_ref[...]
    w_hi = w.astype(jnp.bfloat16)
    w_lo = (w - w_hi.astype(jnp.float32)).astype(jnp.bfloat16)
    logits = _dot(x_hi, w_hi) + (_dot(x_lo, w_hi) + _dot(x_hi, w_lo))
    lane = lax.broadcasted_iota(jnp.int32, logits.shape, 1)
    neg = jnp.float32(-jnp.inf)
    logits = jnp.where(lane < N_EXPERTS, logits, neg)
    m1 = jnp.max(logits, axis=-1, keepdims=True)
    i1 = jnp.min(jnp.where(logits == m1, lane, LANES), axis=-1, keepdims=True)
    rest = jnp.where(lane == i1, neg, logits)
    m2 = jnp.max(rest, axis=-1, keepdims=True)
    i2 = jnp.min(jnp.where(rest == m2, lane, LANES), axis=-1, keepdims=True)
    e2 = jnp.exp(m2 - m1)
    g1 = 1.0 / (1.0 + e2)
    g2 = e2 / (1.0 + e2)
    route = jnp.where(lane == ROUTE_LANE, g1, 0.0) + jnp.where(lane == ROUTE_LANE + 1, g2, 0.0)
    route = route + jnp.where(lane == ROUTE_LANE + 2, i1.astype(jnp.float32), 0.0)
    route = route + jnp.where(lane == ROUTE_LANE + 3, i2.astype(jnp.float32), 0.0)
    gates_ref[...] = route


def _rmsnorm_router(x, gain, router_w, tm=512):
    n, d = x.shape
    wr = jnp.pad(router_w, ((0, 0), (0, LANES - router_w.shape[1])))
    return pl.pallas_call(
        _norm_router_kernel, grid=(n // tm,),
        in_specs=[pl.BlockSpec((tm, d), lambda i: (i, 0)),
                  pl.BlockSpec((1, d), lambda i: (0, 0)),
                  pl.BlockSpec((d, LANES), lambda i: (0, 0))],
        out_specs=[pl.BlockSpec((tm, d), lambda i: (i, 0)),
                   pl.BlockSpec((tm, LANES), lambda i: (i, 0))],
        out_shape=[jax.ShapeDtypeStruct((n, d), jnp.float32),
                   jax.ShapeDtypeStruct((n, LANES), jnp.float32)],
        compiler_params=_cparams("parallel"), name="rmsnorm_router")(x, gain.reshape(1, d), wr)


PROJ_TN = 1024
COL_RQK = 0
COL_RVG = COL_RQK + 2 * R_W
COL_Q = COL_RVG + 2 * R_W
COL_IQ = COL_Q + A_Q
COL_KV = COL_IQ + IDX_Q
COL_IKW = COL_KV + 2 * A_KV
IKW_PAD = PROJ_TN - 2 * A_KV
PROJ_WIDTH = COL_IKW + IKW_PAD
assert COL_Q % PROJ_TN == 0 and COL_KV % PROJ_TN == 0 and PROJ_WIDTH % PROJ_TN == 0


def _proj_kernel(x_ref, g_ref, w_ref, hg_ref, o_ref, ikw_ref, xn_ref):
    j = pl.program_id(1)

    @pl.when(j == 0)
    def _():
        xn_ref[...] = _rms(x_ref, g_ref).astype(xn_ref.dtype)

    acc = _dot(xn_ref[...], w_ref[...])
    o_ref[...] = acc.astype(o_ref.dtype)

    def renorm(n_heads, gain_row):
        for h in range(n_heads):
            y = acc[:, h * A_HEAD_DIM:(h + 1) * A_HEAD_DIM]
            ms = jnp.mean(y * y, axis=-1, keepdims=True)
            y = y * lax.rsqrt(ms + EPS) * hg_ref[gain_row:gain_row + 1, :]
            o_ref[:, h * A_HEAD_DIM:(h + 1) * A_HEAD_DIM] = y.astype(o_ref.dtype)

    @pl.when(jnp.logical_and(j >= COL_Q // PROJ_TN, j < COL_IQ // PROJ_TN))
    def _():
        renorm(PROJ_TN // A_HEAD_DIM, 0)

    @pl.when(j == COL_KV // PROJ_TN)
    def _():
        renorm(A_KV // A_HEAD_DIM, 1)
        off = COL_IKW - COL_KV
        ikw_ref[...] = acc[:, off:off + LANES]


def _project(x, norm_gain, w_all, head_gains, tm=1024):
    n, d = x.shape
    return pl.pallas_call(
        _proj_kernel, grid=(n // tm, PROJ_WIDTH // PROJ_TN),
        in_specs=[pl.BlockSpec((tm, d), lambda i, j: (i, 0)),
                  pl.BlockSpec((1, d), lambda i, j: (0, 0)),
                  pl.BlockSpec((d, PROJ_TN), lambda i, j: (0, j)),
                  pl.BlockSpec((2, A_HEAD_DIM), lambda i, j: (0, 0))],
        out_specs=[pl.BlockSpec((tm, PROJ_TN), lambda i, j: (i, j)),
                   pl.BlockSpec((tm, LANES), lambda i, j: (i, 0))],
        out_shape=[jax.ShapeDtypeStruct((n, PROJ_WIDTH), MXU_DTYPE),
                   jax.ShapeDtypeStruct((n, LANES), jnp.float32)],
        scratch_shapes=[pltpu.VMEM((tm, d), MXU_DTYPE)],
        compiler_params=_cparams("parallel", "arbitrary"), name="in_proj")(
            x, norm_gain.reshape(1, d), w_all, head_gains)


def _merged_in_weights(w_in):
    c0 = A_Q
    c1 = c0 + 2 * A_KV
    c2 = c1 + IDX_Q
    c3 = c2 + IDX_DIM + IDX_HEADS
    c4 = c3 + 2 * R_W
    ikw = jnp.pad(w_in[:, c2:c3], ((0, 0), (0, IKW_PAD - (c3 - c2))))
    parts = [w_in[:, c3:c4], w_in[:, c4:], w_in[:, :c0], w_in[:, c1:c2], w_in[:, c0:c1], ikw]
    return jnp.concatenate(parts, axis=1).astype(MXU_DTYPE)


def _outproj_kernel(a_ref, r_ref, wa_ref, wr_ref, res_ref, o_ref):
    o_ref[...] = res_ref[...] + (_dot(a_ref[...], wa_ref[...]) + _dot(r_ref[...], wr_ref[...]))


def _out_project(a, r, w, res, tm=512, tn=2048):
    n, ka = a.shape
    kr = r.shape[1]
    assert ka == kr
    d = w.shape[1]
    tn = min(tn, d)
    return pl.pallas_call(
        _outproj_kernel, grid=(n // tm, d // tn),
        in_specs=[pl.BlockSpec((tm, ka), lambda i, j: (i, 0)),
                  pl.BlockSpec((tm, kr), lambda i, j: (i, 0)),
                  pl.BlockSpec((ka, tn), lambda i, j: (0, j)),
                  pl.BlockSpec((kr, tn), lambda i, j: (1, j)),
                  pl.BlockSpec((tm, tn), lambda i, j: (i, j))],
        out_specs=pl.BlockSpec((tm, tn), lambda i, j: (i, j)),
        out_shape=jax.ShapeDtypeStruct((n, d), jnp.float32),
        compiler_params=_cparams("parallel", "parallel"), name="out_proj")(a, r, w, w, res)


def _swiglu_hidden(x, wg, wu):
    g = _dot(x, wg)
    u = _dot(x, wu)
    return g * jax.nn.sigmoid(g) * u


def _gu_kernel(x_ref, g_ref, wg_ref, wu_ref, o_ref, xn_ref):
    @pl.when(pl.program_id(1) == 0)
    def _():
        xn_ref[...] = _rms(x_ref, g_ref).astype(xn_ref.dtype)

    o_ref[...] = _swiglu_hidden(xn_ref[...], wg_ref[...], wu_ref[...]).astype(o_ref.dtype)


def _gate_up(x, norm_gain, wg, wu, tm=1024, tn=512):
    n, d = x.shape
    f = wg.shape[1]
    return pl.pallas_call(
        _gu_kernel, grid=(n // tm, f // tn),
        in_specs=[pl.BlockSpec((tm, d), lambda i, j: (i, 0)),
                  pl.BlockSpec((1, d), lambda i, j: (0, 0)),
                  pl.BlockSpec((d, tn), lambda i, j: (0, j)),
                  pl.BlockSpec((d, tn), lambda i, j: (0, j))],
        out_specs=pl.BlockSpec((tm, tn), lambda i, j: (i, j)),
        out_shape=jax.ShapeDtypeStruct((n, f), MXU_DTYPE),
        scratch_shapes=[pltpu.VMEM((tm, d), MXU_DTYPE)],
        compiler_params=_cparams("parallel", "arbitrary"), name="gate_up")(
            x, norm_gain.reshape(1, d), wg, wu)


def _down_kernel(x_ref, w_ref, res_ref, o_ref):
    o_ref[...] = res_ref[...] + _dot(x_ref[...], w_ref[...])


def _down(x, w, res, tm=512, tn=1024):
    n, kt = x.shape
    d = w.shape[1]
    tn = min(tn, d)
    return pl.pallas_call(
        _down_kernel, grid=(d // tn, n // tm),
        in_specs=[pl.BlockSpec((tm, kt), lambda j, i: (i, 0)),
                  pl.BlockSpec((kt, tn), lambda j, i: (0, j)),
                  pl.BlockSpec((tm, tn), lambda j, i: (i, j))],
        out_specs=pl.BlockSpec((tm, tn), lambda j, i: (i, j)),
        out_shape=jax.ShapeDtypeStruct((n, d), jnp.float32),
        compiler_params=_cparams("parallel", "parallel"), name="down_proj")(x, w, res)


def _route_tables(route, tm):
    n = route.shape[0]
    e = route[:, ROUTE_LANE + 2:ROUTE_LANE + 4].astype(jnp.int32).reshape(-1)
    onehot = (e[:, None] == jnp.arange(N_EXPERTS, dtype=jnp.int32)[None, :]).astype(jnp.int32)
    csum = jnp.cumsum(onehot, axis=0)
    rank = jnp.sum(csum * onehot, axis=1) - 1
    tiles_per = (csum[-1] + tm - 1) // tm
    tile_end = jnp.cumsum(tiles_per)
    row_start = (tile_end - tiles_per) * tm
    pos = jnp.sum(onehot * row_start[None, :], axis=1) + rank
    n_tiles = (2 * n) // tm + N_EXPERTS
    t = jnp.arange(n_tiles, dtype=jnp.int32)
    tile_expert = jnp.sum((t[:, None] >= tile_end[None, :]).astype(jnp.int32), axis=1)
    meta = jnp.concatenate([jnp.minimum(tile_expert, N_EXPERTS - 1), tile_end[-1:]]).astype(jnp.int32)
    src = jnp.zeros((n_tiles * tm,), jnp.int32).at[pos].set(
        jnp.arange(2 * n, dtype=jnp.int32) // 2, unique_indices=True)
    return src.reshape(n_tiles, 1, tm), pos.reshape(n, 2), meta


def _row_copy(src_hbm, row, dst, slot, r, sem):
    return pltpu.make_async_copy(src_hbm.at[pl.ds(row, 1)], dst.at[slot, pl.ds(r, 1)], sem)


def _gather_kernel(idx_ref, nxt_ref, x_hbm, o_ref, buf, sem):
    i = pl.program_id(0)
    tm = buf.shape[1]
    slot = lax.rem(i, 2)

    def issue(ref, s):
        def body(k, c):
            for u in range(2):
                r = 2 * k + u
                _row_copy(x_hbm, ref[0, 0, r], buf, s, r, sem.at[s]).start(priority=u)
            return c
        lax.fori_loop(0, tm // 2, body, 0, unroll=4)

    @pl.when(i == 0)
    def _():
        issue(idx_ref, 0)

    @pl.when(i + 1 < pl.num_programs(0))
    def _():
        issue(nxt_ref, 1 - slot)

    pltpu.make_async_copy(x_hbm.at[pl.ds(0, tm)], buf.at[slot], sem.at[slot]).wait()
    o_ref[...] = buf[slot].astype(o_ref.dtype)


def _moe_gather(x, src):
    n_tiles, _, tm = src.shape
    d = x.shape[1]
    idx_spec = pl.BlockSpec((1, 1, tm), lambda i: (i, 0, 0), memory_space=pltpu.SMEM)
    nxt_spec = pl.BlockSpec((1, 1, tm), lambda i: (jnp.minimum(i + 1, n_tiles - 1), 0, 0),
                            memory_space=pltpu.SMEM)
    return pl.pallas_call(
        _gather_kernel, grid=(n_tiles,),
        in_specs=[idx_spec, nxt_spec, pl.BlockSpec(memory_space=pl.ANY)],
        out_specs=pl.BlockSpec((tm, d), lambda i: (i, 0)),
        out_shape=jax.ShapeDtypeStruct((n_tiles * tm, d), MXU_DTYPE),
        scratch_shapes=[pltpu.VMEM((2, tm, d), x.dtype), pltpu.SemaphoreType.DMA((2,))],
        compiler_params=_cparams("arbitrary"), name="moe_gather")(src, src, x)


def _moe_gu_kernel(meta_ref, x_ref, wg_ref, wu_ref, o_ref, wgb_ref, wub_ref):
    i = pl.program_id(1)
    used = i < meta_ref[pl.num_programs(1)]
    new_block = jnp.logical_or(i == 0, meta_ref[i] != meta_ref[jnp.maximum(i - 1, 0)])

    @pl.when(jnp.logical_and(used, new_block))
    def _():
        wgb_ref[...] = wg_ref[0].astype(wgb_ref.dtype)
        wub_ref[...] = wu_ref[0].astype(wub_ref.dtype)

    @pl.when(used)
    def _():
        o_ref[...] = _swiglu_hidden(x_ref[...], wgb_ref[...], wub_ref[...]).astype(o_ref.dtype)

    @pl.when(jnp.logical_not(used))
    def _():
        o_ref[...] = jnp.zeros_like(o_ref)


def _moe_gate_up(xs, wg, wu, meta, tm, tn=1024):
    p, d = xs.shape
    f = wg.shape[2]
    tn = min(tn, f)
    grid_spec = pltpu.PrefetchScalarGridSpec(
        num_scalar_prefetch=1, grid=(f // tn, p // tm),
        in_specs=[pl.BlockSpec((tm, d), lambda j, i, m: (i, 0)),
                  pl.BlockSpec((1, d, tn), lambda j, i, m: (m[i], 0, j)),
                  pl.BlockSpec((1, d, tn), lambda j, i, m: (m[i], 0, j))],
        out_specs=pl.BlockSpec((tm, tn), lambda j, i, m: (i, j)),
        scratch_shapes=[pltpu.VMEM((d, tn), MXU_DTYPE), pltpu.VMEM((d, tn), MXU_DTYPE)])
    return pl.pallas_call(
        _moe_gu_kernel, grid_spec=grid_spec,
        out_shape=jax.ShapeDtypeStruct((p, f), MXU_DTYPE),
        compiler_params=pltpu.CompilerParams(dimension_semantics=("arbitrary", "arbitrary"),
                                             vmem_limit_bytes=MOE_VMEM_LIMIT_BYTES),
        name="moe_gate_up")(meta, xs, wg, wu)


def _moe_down_kernel(meta_ref, x_ref, w_ref, o_ref, wb_ref):
    i = pl.program_id(1)
    used = i < meta_ref[pl.num_programs(1)]
    new_block = jnp.logical_or(i == 0, meta_ref[i] != meta_ref[jnp.maximum(i - 1, 0)])

    @pl.when(jnp.logical_and(used, new_block))
    def _():
        wb_ref[...] = w_ref[0].astype(wb_ref.dtype)

    @pl.when(used)
    def _():
        o_ref[...] = _dot(x_ref[...], wb_ref[...])

    @pl.when(jnp.logical_not(used))
    def _():
        o_ref[...] = jnp.zeros_like(o_ref)


def _moe_down(hid, wd, meta, tm, tn=512):
    p, f = hid.shape
    d = wd.shape[2]
    tn = min(tn, d)
    grid_spec = pltpu.PrefetchScalarGridSpec(
        num_scalar_prefetch=1, grid=(d // tn, p // tm),
        in_specs=[pl.BlockSpec((tm, f), lambda j, i, m: (i, 0)),
                  pl.BlockSpec((1, f, tn), lambda j, i, m: (m[i], 0, j))],
        out_specs=pl.BlockSpec((tm, tn), lambda j, i, m: (i, j)),
        scratch_shapes=[pltpu.VMEM((f, tn), MXU_DTYPE)])
    return pl.pallas_call(
        _moe_down_kernel, grid_spec=grid_spec,
        out_shape=jax.ShapeDtypeStruct((p, d), jnp.float32),
        compiler_params=pltpu.CompilerParams(dimension_semantics=("arbitrary", "arbitrary"),
                                             vmem_limit_bytes=MOE_VMEM_LIMIT_BYTES),
        name="moe_down")(meta, hid, wd)


def _combine_kernel(pa_ref, pb_ref, pa_nxt, pb_nxt, y_hbm, h_ref, route_ref, o_ref, bufa, bufb, sem):
    i = pl.program_id(0)
    tm = bufa.shape[1]
    slot = lax.rem(i, 2)

    def issue(pa, pb, s):
        def body(r, c):
            _row_copy(y_hbm, pa[0, 0, r], bufa, s, r, sem.at[0, s]).start(priority=0)
            _row_copy(y_hbm, pb[0, 0, r], bufb, s, r, sem.at[1, s]).start(priority=1)
            return c
        lax.fori_loop(0, tm, body, 0, unroll=4)

    @pl.when(i == 0)
    def _():
        issue(pa_ref, pb_ref, 0)

    @pl.when(i + 1 < pl.num_programs(0))
    def _():
        issue(pa_nxt, pb_nxt, 1 - slot)

    pltpu.make_async_copy(y_hbm.at[pl.ds(0, tm)], bufa.at[slot], sem.at[0, slot]).wait()
    pltpu.make_async_copy(y_hbm.at[pl.ds(0, tm)], bufb.at[slot], sem.at[1, slot]).wait()
    route = route_ref[...]
    ga = route[:, ROUTE_LANE:ROUTE_LANE + 1]
    gb = route[:, ROUTE_LANE + 1:ROUTE_LANE + 2]
    o_ref[...] = h_ref[...] + (ga * bufa[slot] + gb * bufb[slot])


def _moe_combine(y, pos, h, route, tm):
    n, d = h.shape
    nt = n // tm
    pa = pos[:, 0].reshape(nt, 1, tm)
    pb = pos[:, 1].reshape(nt, 1, tm)
    cur = pl.BlockSpec((1, 1, tm), lambda i: (i, 0, 0), memory_space=pltpu.SMEM)
    nxt = pl.BlockSpec((1, 1, tm), lambda i: (jnp.minimum(i + 1, nt - 1), 0, 0), memory_space=pltpu.SMEM)
    return pl.pallas_call(
        _combine_kernel, grid=(nt,),
        in_specs=[cur, cur, nxt, nxt, pl.BlockSpec(memory_space=pl.ANY),
                  pl.BlockSpec((tm, d), lambda i: (i, 0)),
                  pl.BlockSpec((tm, LANES), lambda i: (i, 0))],
        out_specs=pl.BlockSpec((tm, d), lambda i: (i, 0)),
        out_shape=jax.ShapeDtypeStruct((n, d), jnp.float32),
        scratch_shapes=[pltpu.VMEM((2, tm, d), jnp.float32), pltpu.VMEM((2, tm, d), jnp.float32),
                        pltpu.SemaphoreType.DMA((2, 2))],
        compiler_params=_cparams("arbitrary"), name="moe_combine")(pa, pb, pa, pb, y, h, route)


def _moe(h, norm_gain, router_w, wg, wu, wd):
    hn, route = _rmsnorm_router(h, norm_gain, router_w)
    src, pos, meta = _route_tables(route, MOE_TM)
    xs = _moe_gather(hn, src)
    hid = _moe_gate_up(xs, wg, wu, meta, MOE_TM)
    y = _moe_down(hid, wd, meta, MOE_TM)
    return _moe_combine(y, pos, h, route, COMBINE_TM)


def _sortable(x):
    x = jnp.where(x == 0.0, 0.0, x)
    i = lax.bitcast_convert_type(x, jnp.int32)
    return i ^ ((i >> 31) & jnp.int32(0x7FFFFFFF))


def _dsa_kernel(q_ref, kv_ref, iq_ref, ikk_ref, ikq_ref, nb_ref, o_ref,
                iklo_ref, ikhi_ref, vt_ref, wt_ref, skey_ref, hi_ref, lo_ref, m_ref, l_ref, a_ref,
                acc_ref, s0_ref, s1_ref, p_ref, *, n_top):
    tq = DSA_TQ
    half_keys = tq // 2
    qi = pl.program_id(1)
    f32 = jnp.float32
    nkb = skey_ref.shape[0]

    @pl.when(qi == 0)
    def _():
        ik = ikk_ref[...]
        lane = lax.broadcasted_iota(jnp.int32, ik.shape, 1)
        iklo_ref[...] = jnp.where(lane < IDX_DIM, ik, 0.0).astype(MXU_DTYPE)
        shifted = pltpu.roll(ik, IDX_DIM, axis=1)
        ikhi_ref[...] = jnp.where(lane >= IDX_DIM, shifted, 0.0).astype(MXU_DTYPE)
        for kb in range(nkb):
            for g in range(A_KV_HEADS):
                v = kv_ref[kb * tq:(kb + 1) * tq, A_KV + g * A_HEAD_DIM:A_KV + (g + 1) * A_HEAD_DIM]
                vt_ref[g, kb] = v.astype(f32).T.astype(MXU_DTYPE)

    wt_ref[...] = ikq_ref[...].T

    key_iota = lax.broadcasted_iota(jnp.int32, (half_keys, tq), 0)
    t_chunk = (qi * tq + lax.broadcasted_iota(jnp.int32, (half_keys, tq), 1)) // CHUNK

    def score_block(kb, carry):
        for half in range(2):
            r = pl.multiple_of(kb * tq + half * half_keys, half_keys)
            klo = iklo_ref[pl.ds(r, half_keys), :]
            khi = ikhi_ref[pl.ds(r, half_keys), :]
            acc = jnp.zeros((half_keys, tq), f32)
            for p in range(IDX_HEADS // 2):
                rhs = iq_ref[:, p * LANES:(p + 1) * LANES]
                for lhs, h in ((klo, 2 * p), (khi, 2 * p + 1)):
                    hs = lax.dot_general(lhs, rhs, _NT, preferred_element_type=f32)
                    acc = acc + jnp.maximum(hs, 0.0) * wt_ref[IDX_DIM + h:IDX_DIM + h + 1, :]
            key = _sortable(acc)
            s_chunk = (r + key_iota) // CHUNK
            rows = slice(half * half_keys, (half + 1) * half_keys)
            key = jnp.where(s_chunk <= t_chunk, key, INT_MIN)
            skey_ref[kb, rows, :] = key
            hi_ref[kb, rows, :] = (key >> 16).astype(jnp.int16)
        return carry

    lax.fori_loop(0, qi + 1, score_block, 0)

    n_groups = tq // SUBLANES
    n_acc = 4
    sub_iota = lax.broadcasted_iota(jnp.int32, (SUBLANES, tq), 0)

    def rows8(x):
        return jnp.broadcast_to(x, (SUBLANES, tq))

    def count_where(pred):
        def body(kb, accs):
            accs = list(accs)
            for g in range(n_groups):
                keys = skey_ref[kb, g * SUBLANES:(g + 1) * SUBLANES, :]
                hit = jnp.where(pred(keys, kb * tq + g * SUBLANES), 1.0, 0.0)
                accs[g % n_acc] = accs[g % n_acc] + hit
            return tuple(accs)

        accs = lax.fori_loop(0, qi + 1, body,
                             tuple(jnp.zeros((SUBLANES, tq), f32) for _ in range(n_acc)))
        return jnp.sum((accs[0] + accs[1]) + (accs[2] + accs[3]), axis=0, keepdims=True)

    i16 = jnp.int16
    n_groups16 = tq // PACKED_ROWS

    def rows16(x):
        return jnp.broadcast_to(x, (PACKED_ROWS, tq))

    def count16(ref, pred):
        def body(kb, accs):
            accs = list(accs)
            for g in range(n_groups16):
                rows = ref[kb, g * PACKED_ROWS:(g + 1) * PACKED_ROWS, :]
                hit = jnp.where(pred(rows), i16(1), i16(0))
                accs[g % n_acc] = accs[g % n_acc] + hit
            return tuple(accs)

        accs = lax.fori_loop(0, qi + 1, body,
                             tuple(jnp.zeros((PACKED_ROWS, tq), i16) for _ in range(n_acc)))
        tot = (accs[0] + accs[1]) + (accs[2] + accs[3])
        return jnp.sum(tot.astype(jnp.int32), axis=0, keepdims=True)

    def search16(ref, target):
        def step(it, tu):
            bit = jnp.left_shift(jnp.int32(1), 15 - it)
            cand = rows16(((tu | bit) - HALF16).astype(i16))
            cnt = count16(ref, lambda rows: rows >= cand)
            return jnp.where(cnt >= target, tu | bit, tu)

        return lax.fori_loop(0, 16, step, jnp.zeros((1, tq), jnp.int32))

    t_hi = search16(hi_ref, n_top) - HALF16
    t_hi16 = rows16(t_hi.astype(i16))
    above = count16(hi_ref, lambda rows: rows > t_hi16)
    t_hi8 = rows8(t_hi)

    def low_block(kb, carry):
        for g in range(n_groups16):
            rows = slice(g * PACKED_ROWS, (g + 1) * PACKED_ROWS)
            keys = skey_ref[kb, rows, :]
            t2 = jnp.concatenate([t_hi8, t_hi8], axis=0)
            low = jnp.where((keys >> 16) == t2, (keys & 0xFFFF) - HALF16, -HALF16)
            lo_ref[kb, rows, :] = low.astype(i16)
        return carry

    lax.fori_loop(0, qi + 1, low_block, 0)
    t_lo = search16(lo_ref, n_top - above)
    thr = rows8(t_hi * (2 * HALF16) + t_lo)

    t_lo16 = rows16((t_lo - HALF16).astype(i16))
    n_gt = above + count16(lo_ref, lambda rows: rows > t_lo16)
    n_eq = count16(lo_ref, lambda rows: rows == t_lo16).astype(f32)
    need = (n_top - n_gt).astype(f32)
    n_bits = max(1, int(math.ceil(math.log2(nkb * tq + 1))))

    def tie_search():
        def jstep(it, jb):
            bit = jnp.left_shift(jnp.int32(1), n_bits - 1 - it)
            bound = rows8(jb | bit)
            cnt = count_where(lambda keys, k0: (keys == thr) & ((k0 + sub_iota) < bound))
            return jnp.where(cnt <= need, jb | bit, jb)

        return lax.fori_loop(0, n_bits, jstep, jnp.zeros((1, tq), jnp.int32))

    jb = lax.cond(jnp.max(n_eq - need) > 0.0, tie_search,
                  lambda: jnp.full((1, tq), 2 ** 30, jnp.int32))
    thr_row = thr[:1]
    key_row = lax.broadcasted_iota(jnp.int32, (tq, tq), 0)

    def selection_mask(kb):
        keys = skey_ref[kb]
        tie = (keys == thr_row) & ((kb * tq + key_row) < jb)
        sel = ((keys > thr_row) | tie) & (keys != INT_MIN)
        return jnp.where(sel, 0.0, MASKED).astype(f32)

    m_ref[...] = jnp.full(m_ref.shape, MASKED, f32)
    l_ref[...] = jnp.zeros_like(l_ref)
    acc_ref[...] = jnp.zeros_like(acc_ref)
    grp = A_HEADS // A_KV_HEADS

    def logits(kb_raw, s_ref):
        kb = jnp.minimum(kb_raw, qi)
        r = pl.multiple_of(kb * tq, tq)
        nb_row = pl.multiple_of(jnp.clip(kb - (qi - 2), 0, 2) * tq, tq)
        beyond = jnp.where(kb_raw > qi, MASKED, 0.0).astype(f32)
        mb = selection_mask(kb) + beyond
        for h in range(A_HEADS):
            g = h // grp
            kblk = kv_ref[pl.ds(r, tq), g * A_HEAD_DIM:(g + 1) * A_HEAD_DIM]
            qh = q_ref[:, h * A_HEAD_DIM:(h + 1) * A_HEAD_DIM]
            s = lax.dot_general(kblk, qh, _NT, preferred_element_type=f32)
            s_ref[h] = s + (mb + nb_ref[h, pl.ds(nb_row, tq), :])

    def softmax_pv(kb_raw, s_ref):
        kb = jnp.minimum(kb_raw, qi)
        for h in range(A_HEADS):
            m_prev = m_ref[h]
            m_new = jnp.maximum(m_prev, jnp.max(s_ref[h], axis=0, keepdims=True))
            alpha = jnp.exp2(m_prev - m_new)
            p = jnp.exp2(s_ref[h] - m_new)
            l_ref[h] = alpha * l_ref[h] + jnp.sum(p, axis=0, keepdims=True)
            m_ref[h] = m_new
            a_ref[h] = alpha
            p_ref[h] = p.astype(MXU_DTYPE)
        for h in range(A_HEADS):
            vt = vt_ref[h // grp, kb]
            acc_ref[h] = a_ref[h] * acc_ref[h] + _dot(vt, p_ref[h])

    def attend_pair(j, carry):
        logits(2 * j + 1, s1_ref)
        softmax_pv(2 * j, s0_ref)
        logits(2 * j + 2, s0_ref)
        softmax_pv(2 * j + 1, s1_ref)
        return carry

    logits(0, s0_ref)
    lax.fori_loop(0, (qi + 2) // 2, attend_pair, 0)

    for h in range(A_HEADS):
        o = (acc_ref[h] / l_ref[h]).T
        o_ref[:, h * A_HEAD_DIM:(h + 1) * A_HEAD_DIM] = o.astype(o_ref.dtype)


def _t5_bucket(rel):
    half = REL_BUCKETS // 2
    max_exact = half // 2
    ret = jnp.where(rel > 0, half, 0)
    n = jnp.abs(rel)
    nf = jnp.maximum(n, 1).astype(jnp.float32)
    large = max_exact + (jnp.log(nf / max_exact) / math.log(REL_MAX_DIST / max_exact)
                         * (half - max_exact)).astype(jnp.int32)
    large = jnp.minimum(large, half - 1)
    return ret + jnp.where(n < max_exact, n, large)


def _near_bias(rel_bias):
    tq = DSA_TQ
    c = jnp.arange(2 * tq, dtype=jnp.int32)[:, None]
    i = jnp.arange(tq, dtype=jnp.int32)[None, :]
    rel = c - tq - i
    far = rel_bias[_t5_bucket(jnp.full((1,), -REL_MAX_DIST, jnp.int32))]
    onehot = (_t5_bucket(rel)[None] == jnp.arange(REL_BUCKETS, dtype=jnp.int32)[:, None, None])
    table = (rel_bias - far).T * LOG2E
    near = jnp.einsum('hb,bci->hci', table, onehot.astype(jnp.float32),
                      precision=lax.Precision.HIGHEST)
    return jnp.pad(near, ((0, 0), (tq, 0), (0, 0)))


def _dsa(proj, ikw, near_bias, batch, seq):
    tq = DSA_TQ
    nq = seq // tq
    n_top = min(TOPK_MAX, seq // 4)
    return pl.pallas_call(
        functools.partial(_dsa_kernel, n_top=n_top),
        grid=(batch, nq),
        in_specs=[pl.BlockSpec((tq, A_Q), lambda b, i: (b * nq + i, COL_Q // A_Q)),
                  pl.BlockSpec((seq, 2 * A_KV), lambda b, i: (b, COL_KV // (2 * A_KV))),
                  pl.BlockSpec((tq, IDX_Q), lambda b, i: (b * nq + i, COL_IQ // IDX_Q)),
                  pl.BlockSpec((seq, LANES), lambda b, i: (b, 0)),
                  pl.BlockSpec((tq, LANES), lambda b, i: (b * nq + i, 0)),
                  pl.BlockSpec((A_HEADS, 3 * tq, tq), lambda b, i: (0, 0, 0),
                               pipeline_mode=pl.Buffered(1))],
        out_specs=pl.BlockSpec((tq, A_Q), lambda b, i: (b * nq + i, 0)),
        out_shape=jax.ShapeDtypeStruct((batch * seq, A_Q), MXU_DTYPE),
        scratch_shapes=[pltpu.VMEM((seq, LANES), MXU_DTYPE),
                        pltpu.VMEM((seq, LANES), MXU_DTYPE),
                        pltpu.VMEM((A_KV_HEADS, nq, A_HEAD_DIM, tq), MXU_DTYPE),
                        pltpu.VMEM((LANES, tq), jnp.float32),
                        pltpu.VMEM((nq, tq, tq), jnp.int32),
                        pltpu.VMEM((nq, tq, tq), jnp.int16),
                        pltpu.VMEM((nq, tq, tq), jnp.int16),
                        pltpu.VMEM((A_HEADS, 1, tq), jnp.float32),
                        pltpu.VMEM((A_HEADS, 1, tq), jnp.float32),
                        pltpu.VMEM((A_HEADS, 1, tq), jnp.float32),
                        pltpu.VMEM((A_HEADS, A_HEAD_DIM, tq), jnp.float32),
                        pltpu.VMEM((A_HEADS, tq, tq), jnp.float32),
                        pltpu.VMEM((A_HEADS, tq, tq), jnp.float32),
                        pltpu.VMEM((A_HEADS, tq, tq), MXU_DTYPE)],
        compiler_params=_cparams("arbitrary", "arbitrary"), name="dsa")(
            proj, proj, proj, ikw, ikw, near_bias)


def _ret_kernel(qk_ref, vg_ref, cos_ref, sin_ref, gn_ref, o_ref, state_ref, dm_ref, dz_ref):
    c_len = RET_CHUNK
    f32 = jnp.float32
    hd = R_HEAD_DIM
    half = hd // 2
    log_g = [math.log(1.0 - 2.0 ** (-5.0 - h)) for h in range(R_HEADS)]

    @pl.when(pl.program_id(1) == 0)
    def _():
        state_ref[...] = jnp.zeros_like(state_ref)
        i = lax.broadcasted_iota(jnp.int32, (c_len, c_len), 0)
        j = lax.broadcasted_iota(jnp.int32, (c_len, c_len), 1)
        diff = (i - j).astype(f32)
        row = lax.broadcasted_iota(jnp.int32, (c_len, half), 0).astype(f32)
        for h in range(R_HEADS):
            dm_ref[h] = jnp.where(diff >= 0, jnp.exp(log_g[h] * jnp.maximum(diff, 0.0)), 0.0)
            dz_ref[h, 0] = jnp.exp(log_g[h] * (row + 1.0))
            dz_ref[h, 1] = jnp.exp(log_g[h] * (c_len - 1.0 - row))

    cos = cos_ref[...]
    sin = sin_ref[...]

    def rot(x):
        x1, x2 = x[:, :half], x[:, half:]
        return jnp.concatenate([x1 * cos - x2 * sin, x1 * sin + x2 * cos], axis=1)

    for h in range(R_HEADS):
        cols = slice(h * hd, (h + 1) * hd)
        qf = rot(qk_ref[:, cols].astype(f32))
        kf = rot(qk_ref[:, R_W + h * hd:R_W + (h + 1) * hd].astype(f32)) * (hd ** -0.5)
        v = vg_ref[:, cols]
        att = lax.dot_general(qf.astype(MXU_DTYPE), kf.astype(MXU_DTYPE), _NT,
                              preferred_element_type=f32) * dm_ref[h]
        xi = jnp.concatenate([dz_ref[h, 0]] * 2, axis=1)
        zeta = jnp.concatenate([dz_ref[h, 1]] * 2, axis=1)
        state = state_ref[h]
        o = _dot(att.astype(MXU_DTYPE), v) + _dot((qf * xi).astype(MXU_DTYPE), state.astype(MXU_DTYPE))
        state_ref[h] = state * math.exp(log_g[h] * c_len) + lax.dot_general(
            (kf * zeta).astype(MXU_DTYPE), v, _TN, preferred_element_type=f32)
        mu = jnp.mean(o, axis=-1, keepdims=True)
        d = o - mu
        var = jnp.mean(d * d, axis=-1, keepdims=True)
        y = d * lax.rsqrt(var + GN_EPS) * gn_ref[:, cols]
        g = vg_ref[:, R_W + h * hd:R_W + (h + 1) * hd].astype(f32)
        o_ref[:, cols] = (g * jax.nn.sigmoid(g) * y).astype(o_ref.dtype)


def _rope_tables(seq):
    half = R_HEAD_DIM // 2
    inv = ROPE_BASE ** (-jnp.arange(half, dtype=jnp.float32) / half)
    ang = jnp.arange(seq, dtype=jnp.float32)[:, None] * inv[None, :]
    return jnp.cos(ang), jnp.sin(ang)


def _retention(proj, cos, sin, gn_gain, batch, seq):
    c_len = RET_CHUNK
    nc = seq // c_len
    half = R_HEAD_DIM // 2
    return pl.pallas_call(
        _ret_kernel, grid=(batch, nc),
        in_specs=[pl.BlockSpec((c_len, 2 * R_W), lambda b, c: (b * nc + c, COL_RQK // (2 * R_W))),
                  pl.BlockSpec((c_len, 2 * R_W), lambda b, c: (b * nc + c, COL_RVG // (2 * R_W))),
                  pl.BlockSpec((c_len, half), lambda b, c: (c, 0)),
                  pl.BlockSpec((c_len, half), lambda b, c: (c, 0)),
                  pl.BlockSpec((1, R_W), lambda b, c: (0, 0))],
        out_specs=pl.BlockSpec((c_len, R_W), lambda b, c: (b * nc + c, 0)),
        out_shape=jax.ShapeDtypeStruct((batch * seq, R_W), MXU_DTYPE),
        scratch_shapes=[pltpu.VMEM((R_HEADS, R_HEAD_DIM, R_HEAD_DIM), jnp.float32),
                        pltpu.VMEM((R_HEADS, c_len, c_len), jnp.float32),
                        pltpu.VMEM((R_HEADS, 2, c_len, half), jnp.float32)],
        compiler_params=_cparams("arbitrary", "arbitrary"), name="retention")(
            proj, proj, cos, sin, gn_gain.reshape(1, R_W))


def _mixer(h, norm_gain, w_in, q_gain, k_gain, gn_gain, w_out, near_bias, cos, sin, batch, seq):
    head_gains = jnp.stack([q_gain * (A_HEAD_DIM ** -0.5 * LOG2E), k_gain])
    proj, ikw = _project(h, norm_gain, _merged_in_weights(w_in), head_gains)
    a_out = _dsa(proj, ikw, near_bias, batch, seq)
    r_out = _retention(proj, cos, sin, gn_gain, batch, seq)
    return _out_project(a_out, r_out, w_out.astype(MXU_DTYPE), h)


def kernel(x, rel_bias, norm_mix, w_in, q_gain, k_gain, ret_gain, w_out, norm_ffn,
           ffn_w_gate, ffn_w_up, ffn_w_down, moe_router, moe_w_gate, moe_w_up, moe_w_down):
    batch, seq, d = x.shape
    depth = norm_mix.shape[0]
    h = x.reshape(batch * seq, d)
    near_bias = _near_bias(rel_bias)
    cos, sin = _rope_tables(seq)
    for l in range(depth):
        h = _mixer(h, norm_mix[l], w_in[l], q_gain[l], k_gain[l], ret_gain[l], w_out[l],
                   near_bias, cos, sin, batch, seq)
        j = l // 2
        if l % 2 == 0:
            hid = _gate_up(h, norm_ffn[l], ffn_w_gate[j].astype(MXU_DTYPE),
                           ffn_w_up[j].astype(MXU_DTYPE))
            f = ffn_w_down.shape[1]
            h = _down(hid, ffn_w_down[j].astype(MXU_DTYPE), h)
        else:
            h = _moe(h, norm_ffn[l], moe_router[j], moe_w_gate[j], moe_w_up[j], moe_w_down[j])
    return h.reshape(batch, seq, d)
```

```python
import functools
import math

import jax
import jax.numpy as jnp
import numpy as np
from jax import lax
from jax.experimental import pallas as pl
from jax.experimental.pallas import tpu as pltpu

CHUNK = 64
A_HEADS = 8
A_KV_HEADS = 2
A_HEAD_DIM = 128
IDX_HEADS = 16
IDX_DIM = 64
TOPK_MAX = 256
REL_BUCKETS = 32
REL_MAX_DIST = 128
R_HEADS = 4
R_HEAD_DIM = 256
ROPE_BASE = 10000.0
A_Q = A_HEADS * A_HEAD_DIM
A_KV = A_KV_HEADS * A_HEAD_DIM
IDX_Q = IDX_HEADS * IDX_DIM
R_W = R_HEADS * R_HEAD_DIM
N_EXPERTS = 8
EPS = 1e-6
GN_EPS = 1e-5

LANES = 128
SUBLANES = 8
PACKED_ROWS = 16
HALF16 = 1 << 15
LOG2E = math.log2(math.e)
VMEM_LIMIT_BYTES = 56 * 1024 * 1024
MOE_VMEM_LIMIT_BYTES = 60 * 1024 * 1024
MXU_DTYPE = jnp.bfloat16
DSA_TQ = 256
RET_CHUNK = 256
MOE_TM = 512
COMBINE_TM = 256
ROUTE_LANE = 8
INT_MIN = -2147483648
MASKED = -1e30

_NT = (((1,), (1,)), ((), ()))
_TN = (((0,), (0,)), ((), ()))


def _cparams(*sem):
    return pltpu.CompilerParams(dimension_semantics=sem, vmem_limit_bytes=VMEM_LIMIT_BYTES)


def _dot(a, b):
    return jnp.dot(a, b, preferred_element_type=jnp.float32)


def _rms(x_ref, g_ref):
    x = x_ref[...]
    ms = jnp.mean(x * x, axis=-1, keepdims=True)
    return x * lax.rsqrt(ms + EPS) * g_ref[...]


def _norm_router_kernel(x_ref, g_ref, wr_ref, o_ref, gates_ref):
    xn = _rms(x_ref, g_ref)
    o_ref[...] = xn.astype(o_ref.dtype)
    x_hi = xn.astype(jnp.bfloat16)
    x_lo = (xn - x_hi.astype(jnp.float32)).astype(jnp.bfloat16)
    w = wr_ref[...]
    w_hi = w.astype(jnp.bfloat16)
    w_lo = (w - w_hi.astype(jnp.float32)).astype(jnp.bfloat16)
    logits = _dot(x_hi, w_hi) + (_dot(x_lo, w_hi) + _dot(x_hi, w_lo))
    lane = lax.broadcasted_iota(jnp.int32, logits.shape, 1)
    neg = jnp.float32(-jnp.inf)
    logits = jnp.where(lane < N_EXPERTS, logits, neg)
    m1 = jnp.max(logits, axis=-1, keepdims=True)
    i1 = jnp.min(jnp.where(logits == m1, lane, LANES), axis=-1, keepdims=True)
    rest = jnp.where(lane == i1, neg, logits)
    m2 = jnp.max(rest, axis=-1, keepdims=True)
    i2 = jnp.min(jnp.where(rest == m2, lane, LANES), axis=-1, keepdims=True)
    e2 = jnp.exp(m2 - m1)
    g1 = 1.0 / (1.0 + e2)
    g2 = e2 / (1.0 + e2)
    route = jnp.where(lane == ROUTE_LANE, g1, 0.0) + jnp.where(lane == ROUTE_LANE + 1, g2, 0.0)
    route = route + jnp.where(lane == ROUTE_LANE + 2, i1.astype(jnp.float32), 0.0)
    route = route + jnp.where(lane == ROUTE_LANE + 3, i2.astype(jnp.float32), 0.0)
    gates_ref[...] = route


def _rmsnorm_router(x, gain, router_w, tm=512):
    n, d = x.shape
    wr = jnp.pad(router_w, ((0, 0), (0, LANES - router_w.shape[1])))
    return pl.pallas_call(
        _norm_router_kernel, grid=(n // tm,),
        in_specs=[pl.BlockSpec((tm, d), lambda i: (i, 0)),
                  pl.BlockSpec((1, d), lambda i: (0, 0)),
                  pl.BlockSpec((d, LANES), lambda i: (0, 0))],
        out_specs=[pl.BlockSpec((tm, d), lambda i: (i, 0)),
                   pl.BlockSpec((tm, LANES), lambda i: (i, 0))],
        out_shape=[jax.ShapeDtypeStruct((n, d), jnp.float32),
                   jax.ShapeDtypeStruct((n, LANES), jnp.float32)],
        compiler_params=_cparams("parallel"), name="rmsnorm_router")(x, gain.reshape(1, d), wr)


PROJ_TN = 1024
COL_RQK = 0
COL_RVG = COL_RQK + 2 * R_W
COL_Q = COL_RVG + 2 * R_W
COL_IQ = COL_Q + A_Q
COL_KV = COL_IQ + IDX_Q
COL_IKW = COL_KV + 2 * A_KV
IKW_PAD = PROJ_TN - 2 * A_KV
PROJ_WIDTH = COL_IKW + IKW_PAD
assert COL_Q % PROJ_TN == 0 and COL_KV % PROJ_TN == 0 and PROJ_WIDTH % PROJ_TN == 0


def _proj_kernel(x_ref, g_ref, w_ref, hg_ref, o_ref, ikw_ref, xn_ref):
    j = pl.program_id(1)

    @pl.when(j == 0)
    def _():
        xn_ref[...] = _rms(x_ref, g_ref).astype(xn_ref.dtype)

    acc = _dot(xn_ref[...], w_ref[...])
    o_ref[...] = acc.astype(o_ref.dtype)

    def renorm(n_heads, gain_row):
        for h in range(n_heads):
            y = acc[:, h * A_HEAD_DIM:(h + 1) * A_HEAD_DIM]
            ms = jnp.mean(y * y, axis=-1, keepdims=True)
            y = y * lax.rsqrt(ms + EPS) * hg_ref[gain_row:gain_row + 1, :]
            o_ref[:, h * A_HEAD_DIM:(h + 1) * A_HEAD_DIM] = y.astype(o_ref.dtype)

    @pl.when(jnp.logical_and(j >= COL_Q // PROJ_TN, j < COL_IQ // PROJ_TN))
    def _():
        renorm(PROJ_TN // A_HEAD_DIM, 0)

    @pl.when(j == COL_KV // PROJ_TN)
    def _():
        renorm(A_KV // A_HEAD_DIM, 1)
        off = COL_IKW - COL_KV
        ikw_ref[...] = acc[:, off:off + LANES]


def _project(x, norm_gain, w_all, head_gains, tm=1024):
    n, d = x.shape
    return pl.pallas_call(
        _proj_kernel, grid=(n // tm, PROJ_WIDTH // PROJ_TN),
        in_specs=[pl.BlockSpec((tm, d), lambda i, j: (i, 0)),
                  pl.BlockSpec((1, d), lambda i, j: (0, 0)),
                  pl.BlockSpec((d, PROJ_TN), lambda i, j: (0, j)),
                  pl.BlockSpec((2, A_HEAD_DIM), lambda i, j: (0, 0))],
        out_specs=[pl.BlockSpec((tm, PROJ_TN), lambda i, j: (i, j)),
                   pl.BlockSpec((tm, LANES), lambda i, j: (i, 0))],
        out_shape=[jax.ShapeDtypeStruct((n, PROJ_WIDTH), MXU_DTYPE),
                   jax.ShapeDtypeStruct((n, LANES), jnp.float32)],
        scratch_shapes=[pltpu.VMEM((tm, d), MXU_DTYPE)],
        compiler_params=_cparams("parallel", "arbitrary"), name="in_proj")(
            x, norm_gain.reshape(1, d), w_all, head_gains)


def _merged_in_weights(w_in):
    c0 = A_Q
    c1 = c0 + 2 * A_KV
    c2 = c1 + IDX_Q
    c3 = c2 + IDX_DIM + IDX_HEADS
    c4 = c3 + 2 * R_W
    ikw = jnp.pad(w_in[:, c2:c3], ((0, 0), (0, IKW_PAD - (c3 - c2))))
    parts = [w_in[:, c3:c4], w_in[:, c4:], w_in[:, :c0], w_in[:, c1:c2], w_in[:, c0:c1], ikw]
    return jnp.concatenate(parts, axis=1).astype(MXU_DTYPE)


def _outproj_kernel(a_ref, r_ref, wa_ref, wr_ref, res_ref, o_ref):
    o_ref[...] = res_ref[...] + (_dot(a_ref[...], wa_ref[...]) + _dot(r_ref[...], wr_ref[...]))


def _out_project(a, r, w, res, tm=512, tn=2048):
    n, ka = a.shape
    kr = r.shape[1]
    assert ka == kr
    d = w.shape[1]
    tn = min(tn, d)
    return pl.pallas_call(
        _outproj_kernel, grid=(n // tm, d // tn),
        in_specs=[pl.BlockSpec((tm, ka), lambda i, j: (i, 0)),
                  pl.BlockSpec((tm, kr), lambda i, j: (i, 0)),
                  pl.BlockSpec((ka, tn), lambda i, j: (0, j)),
                  pl.BlockSpec((kr, tn), lambda i, j: (1, j)),
                  pl.BlockSpec((tm, tn), lambda i, j: (i, j))],
        out_specs=pl.BlockSpec((tm, tn), lambda i, j: (i, j)),
        out_shape=jax.ShapeDtypeStruct((n, d), jnp.float32),
        compiler_params=_cparams("parallel", "parallel"), name="out_proj")(a, r, w, w, res)


def _swiglu_hidden(x, wg, wu):
    g = _dot(x, wg)
    u = _dot(x, wu)
    return g * jax.nn.sigmoid(g) * u


def _gu_kernel(x_ref, g_ref, wg_ref, wu_ref, o_ref, xn_ref):
    @pl.when(pl.program_id(1) == 0)
    def _():
        xn_ref[...] = _rms(x_ref, g_ref).astype(xn_ref.dtype)

    o_ref[...] = _swiglu_hidden(xn_ref[...], wg_ref[...], wu_ref[...]).astype(o_ref.dtype)


def _gate_up(x, norm_gain, wg, wu, tm=1024, tn=512):
    n, d = x.shape
    f = wg.shape[1]
    return pl.pallas_call(
        _gu_kernel, grid=(n // tm, f // tn),
        in_specs=[pl.BlockSpec((tm, d), lambda i, j: (i, 0)),
                  pl.BlockSpec((1, d), lambda i, j: (0, 0)),
                  pl.BlockSpec((d, tn), lambda i, j: (0, j)),
                  pl.BlockSpec((d, tn), lambda i, j: (0, j))],
        out_specs=pl.BlockSpec((tm, tn), lambda i, j: (i, j)),
        out_shape=jax.ShapeDtypeStruct((n, f), MXU_DTYPE),
        scratch_shapes=[pltpu.VMEM((tm, d), MXU_DTYPE)],
        compiler_params=_cparams("parallel", "arbitrary"), name="gate_up")(
            x, norm_gain.reshape(1, d), wg, wu)


def _down_kernel(x_ref, w_ref, res_ref, o_ref):
    o_ref[...] = res_ref[...] + _dot(x_ref[...], w_ref[...])


def _down(x, w, res, tm=512, tn=1024):
    n, kt = x.shape
    d = w.shape[1]
    tn = min(tn, d)
    return pl.pallas_call(
        _down_kernel, grid=(d // tn, n // tm),
        in_specs=[pl.BlockSpec((tm, kt), lambda j, i: (i, 0)),
                  pl.BlockSpec((kt, tn), lambda j, i: (0, j)),
                  pl.BlockSpec((tm, tn), lambda j, i: (i, j))],
        out_specs=pl.BlockSpec((tm, tn), lambda j, i: (i, j)),
        out_shape=jax.ShapeDtypeStruct((n, d), jnp.float32),
        compiler_params=_cparams("parallel", "parallel"), name="down_proj")(x, w, res)


def _route_tables(route, tm):
    n = route.shape[0]
    e = route[:, ROUTE_LANE + 2:ROUTE_LANE + 4].astype(jnp.int32).reshape(-1)
    onehot = (e[:, None] == jnp.arange(N_EXPERTS, dtype=jnp.int32)[None, :]).astype(jnp.int32)
    csum = jnp.cumsum(onehot, axis=0)
    rank = jnp.sum(csum * onehot, axis=1) - 1
    tiles_per = (csum[-1] + tm - 1) // tm
    tile_end = jnp.cumsum(tiles_per)
    row_start = (tile_end - tiles_per) * tm
    pos = jnp.sum(onehot * row_start[None, :], axis=1) + rank
    n_tiles = (2 * n) // tm + N_EXPERTS
    t = jnp.arange(n_tiles, dtype=jnp.int32)
    tile_expert = jnp.sum((t[:, None] >= tile_end[None, :]).astype(jnp.int32), axis=1)
    meta = jnp.concatenate([jnp.minimum(tile_expert, N_EXPERTS - 1), tile_end[-1:]]).astype(jnp.int32)
    src = jnp.zeros((n_tiles * tm,), jnp.int32).at[pos].set(
        jnp.arange(2 * n, dtype=jnp.int32) // 2, unique_indices=True)
    return src.reshape(n_tiles, 1, tm), pos.reshape(n, 2), meta


def _row_copy(src_hbm, row, dst, slot, r, sem):
    return pltpu.make_async_copy(src_hbm.at[pl.ds(row, 1)], dst.at[slot, pl.ds(r, 1)], sem)


def _gather_kernel(idx_ref, nxt_ref, x_hbm, o_ref, buf, sem):
    i = pl.program_id(0)
    tm = buf.shape[1]
    slot = lax.rem(i, 2)

    def issue(ref, s):
        def body(k, c):
            for u in range(2):
                r = 2 * k + u
                _row_copy(x_hbm, ref[0, 0, r], buf, s, r, sem.at[s]).start(priority=u)
            return c
        lax.fori_loop(0, tm // 2, body, 0, unroll=4)

    @pl.when(i == 0)
    def _():
        issue(idx_ref, 0)

    @pl.when(i + 1 < pl.num_programs(0))
    def _():
        issue(nxt_ref, 1 - slot)

    pltpu.make_async_copy(x_hbm.at[pl.ds(0, tm)], buf.at[slot], sem.at[slot]).wait()
    o_ref[...] = buf[slot].astype(o_ref.dtype)


def _moe_gather(x, src):
    n_tiles, _, tm = src.shape
    d = x.shape[1]
    idx_spec = pl.BlockSpec((1, 1, tm), lambda i: (i, 0, 0), memory_space=pltpu.SMEM)
    nxt_spec = pl.BlockSpec((1, 1, tm), lambda i: (jnp.minimum(i + 1, n_tiles - 1), 0, 0),
                            memory_space=pltpu.SMEM)
    return pl.pallas_call(
        _gather_kernel, grid=(n_tiles,),
        in_specs=[idx_spec, nxt_spec, pl.BlockSpec(memory_space=pl.ANY)],
        out_specs=pl.BlockSpec((tm, d), lambda i: (i, 0)),
        out_shape=jax.ShapeDtypeStruct((n_tiles * tm, d), MXU_DTYPE),
        scratch_shapes=[pltpu.VMEM((2, tm, d), x.dtype), pltpu.SemaphoreType.DMA((2,))],
        compiler_params=_cparams("arbitrary"), name="moe_gather")(src, src, x)


def _moe_gu_kernel(meta_ref, x_ref, wg_ref, wu_ref, o_ref, wgb_ref, wub_ref):
    i = pl.program_id(1)
    used = i < meta_ref[pl.num_programs(1)]
    new_block = jnp.logical_or(i == 0, meta_ref[i] != meta_ref[jnp.maximum(i - 1, 0)])

    @pl.when(jnp.logical_and(used, new_block))
    def _():
        wgb_ref[...] = wg_ref[0].astype(wgb_ref.dtype)
        wub_ref[...] = wu_ref[0].astype(wub_ref.dtype)

    @pl.when(used)
    def _():
        o_ref[...] = _swiglu_hidden(x_ref[...], wgb_ref[...], wub_ref[...]).astype(o_ref.dtype)

    @pl.when(jnp.logical_not(used))
    def _():
        o_ref[...] = jnp.zeros_like(o_ref)


def _moe_gate_up(xs, wg, wu, meta, tm, tn=1024):
    p, d = xs.shape
    f = wg.shape[2]
    tn = min(tn, f)
    grid_spec = pltpu.PrefetchScalarGridSpec(
        num_scalar_prefetch=1, grid=(f // tn, p // tm),
        in_specs=[pl.BlockSpec((tm, d), lambda j, i, m: (i, 0)),
                  pl.BlockSpec((1, d, tn), lambda j, i, m: (m[i], 0, j)),
                  pl.BlockSpec((1, d, tn), lambda j, i, m: (m[i], 0, j))],
        out_specs=pl.BlockSpec((tm, tn), lambda j, i, m: (i, j)),
        scratch_shapes=[pltpu.VMEM((d, tn), MXU_DTYPE), pltpu.VMEM((d, tn), MXU_DTYPE)])
    return pl.pallas_call(
        _moe_gu_kernel, grid_spec=grid_spec,
        out_shape=jax.ShapeDtypeStruct((p, f), MXU_DTYPE),
        compiler_params=pltpu.CompilerParams(dimension_semantics=("arbitrary", "arbitrary"),
                                             vmem_limit_bytes=MOE_VMEM_LIMIT_BYTES),
        name="moe_gate_up")(meta, xs, wg, wu)


def _moe_down_kernel(meta_ref, x_ref, w_ref, o_ref, wb_ref):
    i = pl.program_id(1)
    used = i < meta_ref[pl.num_programs(1)]
    new_block = jnp.logical_or(i == 0, meta_ref[i] != meta_ref[jnp.maximum(i - 1, 0)])

    @pl.when(jnp.logical_and(used, new_block))
    def _():
        wb_ref[...] = w_ref[0].astype(wb_ref.dtype)

    @pl.when(used)
    def _():
        o_ref[...] = _dot(x_ref[...], wb_ref[...])

    @pl.when(jnp.logical_not(used))
    def _():
        o_ref[...] = jnp.zeros_like(o_ref)


def _moe_down(hid, wd, meta, tm, tn=512):
    p, f = hid.shape
    d = wd.shape[2]
    tn = min(tn, d)
    grid_spec = pltpu.PrefetchScalarGridSpec(
        num_scalar_prefetch=1, grid=(d // tn, p // tm),
        in_specs=[pl.BlockSpec((tm, f), lambda j, i, m: (i, 0)),
                  pl.BlockSpec((1, f, tn), lambda j, i, m: (m[i], 0, j))],
        out_specs=pl.BlockSpec((tm, tn), lambda j, i, m: (i, j)),
        scratch_shapes=[pltpu.VMEM((f, tn), MXU_DTYPE)])
    return pl.pallas_call(
        _moe_down_kernel, grid_spec=grid_spec,
        out_shape=jax.ShapeDtypeStruct((p, d), jnp.float32),
        compiler_params=pltpu.CompilerParams(dimension_semantics=("arbitrary", "arbitrary"),
                                             vmem_limit_bytes=MOE_VMEM_LIMIT_BYTES),
        name="moe_down")(meta, hid, wd)


def _combine_kernel(pa_ref, pb_ref, pa_nxt, pb_nxt, y_hbm, h_ref, route_ref, o_ref, bufa, bufb, sem):
    i = pl.program_id(0)
    tm = bufa.shape[1]
    slot = lax.rem(i, 2)

    def issue(pa, pb, s):
        def body(r, c):
            _row_copy(y_hbm, pa[0, 0, r], bufa, s, r, sem.at[0, s]).start(priority=0)
            _row_copy(y_hbm, pb[0, 0, r], bufb, s, r, sem.at[1, s]).start(priority=1)
            return c
        lax.fori_loop(0, tm, body, 0, unroll=4)

    @pl.when(i == 0)
    def _():
        issue(pa_ref, pb_ref, 0)

    @pl.when(i + 1 < pl.num_programs(0))
    def _():
        issue(pa_nxt, pb_nxt, 1 - slot)

    pltpu.make_async_copy(y_hbm.at[pl.ds(0, tm)], bufa.at[slot], sem.at[0, slot]).wait()
    pltpu.make_async_copy(y_hbm.at[pl.ds(0, tm)], bufb.at[slot], sem.at[1, slot]).wait()
    route = route_ref[...]
    ga = route[:, ROUTE_LANE:ROUTE_LANE + 1]
    gb = route[:, ROUTE_LANE + 1:ROUTE_LANE + 2]
    o_ref[...] = h_ref[...] + (ga * bufa[slot] + gb * bufb[slot])


def _moe_combine(y, pos, h, route, tm):
    n, d = h.shape
    nt = n // tm
    pa = pos[:, 0].reshape(nt, 1, tm)
    pb = pos[:, 1].reshape(nt, 1, tm)
    cur = pl.BlockSpec((1, 1, tm), lambda i: (i, 0, 0), memory_space=pltpu.SMEM)
    nxt = pl.BlockSpec((1, 1, tm), lambda i: (jnp.minimum(i + 1, nt - 1), 0, 0), memory_space=pltpu.SMEM)
    return pl.pallas_call(
        _combine_kernel, grid=(nt,),
        in_specs=[cur, cur, nxt, nxt, pl.BlockSpec(memory_space=pl.ANY),
                  pl.BlockSpec((tm, d), lambda i: (i, 0)),
                  pl.BlockSpec((tm, LANES), lambda i: (i, 0))],
        out_specs=pl.BlockSpec((tm, d), lambda i: (i, 0)),
        out_shape=jax.ShapeDtypeStruct((n, d), jnp.float32),
        scratch_shapes=[pltpu.VMEM((2, tm, d), jnp.float32), pltpu.VMEM((2, tm, d), jnp.float32),
                        pltpu.SemaphoreType.DMA((2, 2))],
        compiler_params=_cparams("arbitrary"), name="moe_combine")(pa, pb, pa, pb, y, h, route)


def _moe(h, norm_gain, router_w, wg, wu, wd):
    hn, route = _rmsnorm_router(h, norm_gain, router_w)
    src, pos, meta = _route_tables(route, MOE_TM)
    xs = _moe_gather(hn, src)
    hid = _moe_gate_up(xs, wg, wu, meta, MOE_TM)
    y = _moe_down(hid, wd, meta, MOE_TM)
    return _moe_combine(y, pos, h, route, COMBINE_TM)


def _sortable(x):
    x = jnp.where(x == 0.0, 0.0, x)
    i = lax.bitcast_convert_type(x, jnp.int32)
    return i ^ ((i >> 31) & jnp.int32(0x7FFFFFFF))


def _dsa_kernel(q_ref, kv_ref, iq_ref, ikk_ref, ikq_ref, nb_ref, o_ref,
                iklo_ref, ikhi_ref, vt_ref, wt_ref, skey_ref, hi_ref, lo_ref, m_ref, l_ref, a_ref,
                acc_ref, s0_ref, s1_ref, p_ref, *, n_top):
    tq = DSA_TQ
    half_keys = tq // 2
    qi = pl.program_id(1)
    f32 = jnp.float32
    nkb = skey_ref.shape[0]

    @pl.when(qi == 0)
    def _():
        ik = ikk_ref[...]
        lane = lax.broadcasted_iota(jnp.int32, ik.shape, 1)
        iklo_ref[...] = jnp.where(lane < IDX_DIM, ik, 0.0).astype(MXU_DTYPE)
        shifted = pltpu.roll(ik, IDX_DIM, axis=1)
        ikhi_ref[...] = jnp.where(lane >= IDX_DIM, shifted, 0.0).astype(MXU_DTYPE)
        for kb in range(nkb):
            for g in range(A_KV_HEADS):
                v = kv_ref[kb * tq:(kb + 1) * tq, A_KV + g * A_HEAD_DIM:A_KV + (g + 1) * A_HEAD_DIM]
                vt_ref[g, kb] = v.astype(f32).T.astype(MXU_DTYPE)

    wt_ref[...] = ikq_ref[...].T

    key_iota = lax.broadcasted_iota(jnp.int32, (half_keys, tq), 0)
    t_chunk = (qi * tq + lax.broadcasted_iota(jnp.int32, (half_keys, tq), 1)) // CHUNK

    def score_block(kb, carry):
        for half in range(2):
            r = pl.multiple_of(kb * tq + half * half_keys, half_keys)
            klo = iklo_ref[pl.ds(r, half_keys), :]
            khi = ikhi_ref[pl.ds(r, half_keys), :]
            acc = jnp.zeros((half_keys, tq), f32)
            for p in range(IDX_HEADS // 2):
                rhs = iq_ref[:, p * LANES:(p + 1) * LANES]
                for lhs, h in ((klo, 2 * p), (khi, 2 * p + 1)):
                    hs = lax.dot_general(lhs, rhs, _NT, preferred_element_type=f32)
                    acc = acc + jnp.maximum(hs, 0.0) * wt_ref[IDX_DIM + h:IDX_DIM + h + 1, :]
            key = _sortable(acc)
            s_chunk = (r + key_iota) // CHUNK
            rows = slice(half * half_keys, (half + 1) * half_keys)
            key = jnp.where(s_chunk <= t_chunk, key, INT_MIN)
            skey_ref[kb, rows, :] = key
            hi_ref[kb, rows, :] = (key >> 16).astype(jnp.int16)
        return carry

    lax.fori_loop(0, qi + 1, score_block, 0)

    n_groups = tq // SUBLANES
    n_acc = 4
    sub_iota = lax.broadcasted_iota(jnp.int32, (SUBLANES, tq), 0)

    def rows8(x):
        return jnp.broadcast_to(x, (SUBLANES, tq))

    def count_where(pred):
        def body(kb, accs):
            accs = list(accs)
            for g in range(n_groups):
                keys = skey_ref[kb, g * SUBLANES:(g + 1) * SUBLANES, :]
                hit = jnp.where(pred(keys, kb * tq + g * SUBLANES), 1.0, 0.0)
                accs[g % n_acc] = accs[g % n_acc] + hit
            return tuple(accs)

        accs = lax.fori_loop(0, qi + 1, body,
                             tuple(jnp.zeros((SUBLANES, tq), f32) for _ in range(n_acc)))
        return jnp.sum((accs[0] + accs[1]) + (accs[2] + accs[3]), axis=0, keepdims=True)

    i16 = jnp.int16
    n_groups16 = tq // PACKED_ROWS

    def rows16(x):
        return jnp.broadcast_to(x, (PACKED_ROWS, tq))

    def count16(ref, pred):
        def body(kb, accs):
            accs = list(accs)
            for g in range(n_groups16):
                rows = ref[kb, g * PACKED_ROWS:(g + 1) * PACKED_ROWS, :]
                hit = jnp.where(pred(rows), i16(1), i16(0))
                accs[g % n_acc] = accs[g % n_acc] + hit
            return tuple(accs)

        accs = lax.fori_loop(0, qi + 1, body,
                             tuple(jnp.zeros((PACKED_ROWS, tq), i16) for _ in range(n_acc)))
        tot = (accs[0] + accs[1]) + (accs[2] + accs[3])
        return jnp.sum(tot.astype(jnp.int32), axis=0, keepdims=True)

    def search16(ref, target):
        def step(it, tu):
            bit = jnp.left_shift(jnp.int32(1), 15 - it)
            cand = rows16(((tu | bit) - HALF16).astype(i16))
            cnt = count16(ref, lambda rows: rows >= cand)
            return jnp.where(cnt >= target, tu | bit, tu)

        return lax.fori_loop(0, 16, step, jnp.zeros((1, tq), jnp.int32))

    t_hi = search16(hi_ref, n_top) - HALF16
    t_hi16 = rows16(t_hi.astype(i16))
    above = count16(hi_ref, lambda rows: rows > t_hi16)
    t_hi8 = rows8(t_hi)

    def low_block(kb, carry):
        for g in range(n_groups16):
            rows = slice(g * PACKED_ROWS, (g + 1) * PACKED_ROWS)
            keys = skey_ref[kb, rows, :]
            t2 = jnp.concatenate([t_hi8, t_hi8], axis=0)
            low = jnp.where((keys >> 16) == t2, (keys & 0xFFFF) - HALF16, -HALF16)
            lo_ref[kb, rows, :] = low.astype(i16)
        return carry

    lax.fori_loop(0, qi + 1, low_block, 0)
    t_lo = search16(lo_ref, n_top - above)
    thr = rows8(t_hi * (2 * HALF16) + t_lo)

    t_lo16 = rows16((t_lo - HALF16).astype(i16))
    n_gt = above + count16(lo_ref, lambda rows: rows > t_lo16)
    n_eq = count16(lo_ref, lambda rows: rows == t_lo16).astype(f32)
    need = (n_top - n_gt).astype(f32)
    n_bits = max(1, int(math.ceil(math.log2(nkb * tq + 1))))

    def tie_search():
        def jstep(it, jb):
            bit = jnp.left_shift(jnp.int32(1), n_bits - 1 - it)
            bound = rows8(jb | bit)
            cnt = count_where(lambda keys, k0: (keys == thr) & ((k0 + sub_iota) < bound))
            return jnp.where(cnt <= need, jb | bit, jb)

        return lax.fori_loop(0, n_bits, jstep, jnp.zeros((1, tq), jnp.int32))

    jb = lax.cond(jnp.max(n_eq - need) > 0.0, tie_search,
                  lambda: jnp.full((1, tq), 2 ** 30, jnp.int32))
    thr_row = thr[:1]
    key_row = lax.broadcasted_iota(jnp.int32, (tq, tq), 0)

    def selection_mask(kb):
        keys = skey_ref[kb]
        tie = (keys == thr_row) & ((kb * tq + key_row) < jb)
        sel = ((keys > thr_row) | tie) & (keys != INT_MIN)
        return jnp.where(sel, 0.0, MASKED).astype(f32)

    m_ref[...] = jnp.full(m_ref.shape, MASKED, f32)
    l_ref[...] = jnp.zeros_like(l_ref)
    acc_ref[...] = jnp.zeros_like(acc_ref)
    grp = A_HEADS // A_KV_HEADS

    def logits(kb_raw, s_ref):
        kb = jnp.minimum(kb_raw, qi)
        r = pl.multiple_of(kb * tq, tq)
        nb_row = pl.multiple_of(jnp.clip(kb - (qi - 2), 0, 2) * tq, tq)
        beyond = jnp.where(kb_raw > qi, MASKED, 0.0).astype(f32)
        mb = selection_mask(kb) + beyond
        for h in range(A_HEADS):
            g = h // grp
            kblk = kv_ref[pl.ds(r, tq), g * A_HEAD_DIM:(g + 1) * A_HEAD_DIM]
            qh = q_ref[:, h * A_HEAD_DIM:(h + 1) * A_HEAD_DIM]
            s = lax.dot_general(kblk, qh, _NT, preferred_element_type=f32)
            s_ref[h] = s + (mb + nb_ref[h, pl.ds(nb_row, tq), :])

    def softmax_pv(kb_raw, s_ref):
        kb = jnp.minimum(kb_raw, qi)
        for h in range(A_HEADS):
            m_prev = m_ref[h]
            m_new = jnp.maximum(m_prev, jnp.max(s_ref[h], axis=0, keepdims=True))
            alpha = jnp.exp2(m_prev - m_new)
            p = jnp.exp2(s_ref[h] - m_new)
            l_ref[h] = alpha * l_ref[h] + jnp.sum(p, axis=0, keepdims=True)
            m_ref[h] = m_new
            a_ref[h] = alpha
            p_ref[h] = p.astype(MXU_DTYPE)
        for h in range(A_HEADS):
            vt = vt_ref[h // grp, kb]
            acc_ref[h] = a_ref[h] * acc_ref[h] + _dot(vt, p_ref[h])

    def attend_pair(j, carry):
        logits(2 * j + 1, s1_ref)
        softmax_pv(2 * j, s0_ref)
        logits(2 * j + 2, s0_ref)
        softmax_pv(2 * j + 1, s1_ref)
        return carry

    logits(0, s0_ref)
    n_blocks = qi + 1
    lax.fori_loop(0, lax.shift_right_logical(n_blocks, 1), attend_pair, 0)

    @pl.when(lax.rem(n_blocks, 2) == 1)
    def _():
        softmax_pv(qi, s0_ref)

    for h in range(A_HEADS):
        o = (acc_ref[h] / l_ref[h]).T
        o_ref[:, h * A_HEAD_DIM:(h + 1) * A_HEAD_DIM] = o.astype(o_ref.dtype)


def _t5_bucket(rel):
    half = REL_BUCKETS // 2
    max_exact = half // 2
    ret = jnp.where(rel > 0, half, 0)
    n = jnp.abs(rel)
    nf = jnp.maximum(n, 1).astype(jnp.float32)
    large = max_exact + (jnp.log(nf / max_exact) / math.log(REL_MAX_DIST / max_exact)
                         * (half - max_exact)).astype(jnp.int32)
    large = jnp.minimum(large, half - 1)
    return ret + jnp.where(n < max_exact, n, large)


def _near_bias(rel_bias):
    tq = DSA_TQ
    c = jnp.arange(2 * tq, dtype=jnp.int32)[:, None]
    i = jnp.arange(tq, dtype=jnp.int32)[None, :]
    rel = c - tq - i
    far = rel_bias[_t5_bucket(jnp.full((1,), -REL_MAX_DIST, jnp.int32))]
    onehot = (_t5_bucket(rel)[None] == jnp.arange(REL_BUCKETS, dtype=jnp.int32)[:, None, None])
    table = (rel_bias - far).T * LOG2E
    near = jnp.einsum('hb,bci->hci', table, onehot.astype(jnp.float32),
                      precision=lax.Precision.HIGHEST)
    return jnp.pad(near, ((0, 0), (tq, 0), (0, 0)))


def _dsa(proj, ikw, near_bias, batch, seq):
    tq = DSA_TQ
    nq = seq // tq
    n_top = min(TOPK_MAX, seq // 4)
    return pl.pallas_call(
        functools.partial(_dsa_kernel, n_top=n_top),
        grid=(batch, nq),
        in_specs=[pl.BlockSpec((tq, A_Q), lambda b, i: (b * nq + i, COL_Q // A_Q)),
                  pl.BlockSpec((seq, 2 * A_KV), lambda b, i: (b, COL_KV // (2 * A_KV))),
                  pl.BlockSpec((tq, IDX_Q), lambda b, i: (b * nq + i, COL_IQ // IDX_Q)),
                  pl.BlockSpec((seq, LANES), lambda b, i: (b, 0)),
                  pl.BlockSpec((tq, LANES), lambda b, i: (b * nq + i, 0)),
                  pl.BlockSpec((A_HEADS, 3 * tq, tq), lambda b, i: (0, 0, 0),
                               pipeline_mode=pl.Buffered(1))],
        out_specs=pl.BlockSpec((tq, A_Q), lambda b, i: (b * nq + i, 0)),
        out_shape=jax.ShapeDtypeStruct((batch * seq, A_Q), MXU_DTYPE),
        scratch_shapes=[pltpu.VMEM((seq, LANES), MXU_DTYPE),
                        pltpu.VMEM((seq, LANES), MXU_DTYPE),
                        pltpu.VMEM((A_KV_HEADS, nq, A_HEAD_DIM, tq), MXU_DTYPE),
                        pltpu.VMEM((LANES, tq), jnp.float32),
                        pltpu.VMEM((nq, tq, tq), jnp.int32),
                        pltpu.VMEM((nq, tq, tq), jnp.int16),
                        pltpu.VMEM((nq, tq, tq), jnp.int16),
                        pltpu.VMEM((A_HEADS, 1, tq), jnp.float32),
                        pltpu.VMEM((A_HEADS, 1, tq), jnp.float32),
                        pltpu.VMEM((A_HEADS, 1, tq), jnp.float32),
                        pltpu.VMEM((A_HEADS, A_HEAD_DIM, tq), jnp.float32),
                        pltpu.VMEM((A_HEADS, tq, tq), jnp.float32),
                        pltpu.VMEM((A_HEADS, tq, tq), jnp.float32),
                        pltpu.VMEM((A_HEADS, tq, tq), MXU_DTYPE)],
        compiler_params=_cparams("arbitrary", "arbitrary"), name="dsa")(
            proj, proj, proj, ikw, ikw, near_bias)


def _ret_kernel(qk_ref, vg_ref, cos_ref, sin_ref, gn_ref, o_ref, state_ref, dm_ref, dz_ref):
    c_len = RET_CHUNK
    f32 = jnp.float32
    hd = R_HEAD_DIM
    half = hd // 2
    log_g = [math.log(1.0 - 2.0 ** (-5.0 - h)) for h in range(R_HEADS)]

    @pl.when(pl.program_id(1) == 0)
    def _():
        state_ref[...] = jnp.zeros_like(state_ref)
        i = lax.broadcasted_iota(jnp.int32, (c_len, c_len), 0)
        j = lax.broadcasted_iota(jnp.int32, (c_len, c_len), 1)
        diff = (i - j).astype(f32)
        row = lax.broadcasted_iota(jnp.int32, (c_len, half), 0).astype(f32)
        for h in range(R_HEADS):
            dm_ref[h] = jnp.where(diff >= 0, jnp.exp(log_g[h] * jnp.maximum(diff, 0.0)), 0.0)
            dz_ref[h, 0] = jnp.exp(log_g[h] * (row + 1.0))
            dz_ref[h, 1] = jnp.exp(log_g[h] * (c_len - 1.0 - row))

    cos = cos_ref[...]
    sin = sin_ref[...]

    def rot(x):
        x1, x2 = x[:, :half], x[:, half:]
        return jnp.concatenate([x1 * cos - x2 * sin, x1 * sin + x2 * cos], axis=1)

    for h in range(R_HEADS):
        cols = slice(h * hd, (h + 1) * hd)
        qf = rot(qk_ref[:, cols].astype(f32))
        kf = rot(qk_ref[:, R_W + h * hd:R_W + (h + 1) * hd].astype(f32)) * (hd ** -0.5)
        v = vg_ref[:, cols]
        att = lax.dot_general(qf.astype(MXU_DTYPE), kf.astype(MXU_DTYPE), _NT,
                              preferred_element_type=f32) * dm_ref[h]
        xi = jnp.concatenate([dz_ref[h, 0]] * 2, axis=1)
        zeta = jnp.concatenate([dz_ref[h, 1]] * 2, axis=1)
        state = state_ref[h]
        o = _dot(att.astype(MXU_DTYPE), v) + _dot((qf * xi).astype(MXU_DTYPE), state.astype(MXU_DTYPE))
        state_ref[h] = state * math.exp(log_g[h] * c_len) + lax.dot_general(
            (kf * zeta).astype(MXU_DTYPE), v, _TN, preferred_element_type=f32)
        mu = jnp.mean(o, axis=-1, keepdims=True)
        d = o - mu
        var = jnp.mean(d * d, axis=-1, keepdims=True)
        y = d * lax.rsqrt(var + GN_EPS) * gn_ref[:, cols]
        g = vg_ref[:, R_W + h * hd:R_W + (h + 1) * hd].astype(f32)
        o_ref[:, cols] = (g * jax.nn.sigmoid(g) * y).astype(o_ref.dtype)


def _rope_tables(seq):
    half = R_HEAD_DIM // 2
    inv = ROPE_BASE ** (-jnp.arange(half, dtype=jnp.float32) / half)
    ang = jnp.arange(seq, dtype=jnp.float32)[:, None] * inv[None, :]
    return jnp.cos(ang), jnp.sin(ang)


def _retention(proj, cos, sin, gn_gain, batch, seq):
    c_len = RET_CHUNK
    nc = seq // c_len
    half = R_HEAD_DIM // 2
    return pl.pallas_call(
        _ret_kernel, grid=(batch, nc),
        in_specs=[pl.BlockSpec((c_len, 2 * R_W), lambda b, c: (b * nc + c, COL_RQK // (2 * R_W))),
                  pl.BlockSpec((c_len, 2 * R_W), lambda b, c: (b * nc + c, COL_RVG // (2 * R_W))),
                  pl.BlockSpec((c_len, half), lambda b, c: (c, 0)),
                  pl.BlockSpec((c_len, half), lambda b, c: (c, 0)),
                  pl.BlockSpec((1, R_W), lambda b, c: (0, 0))],
        out_specs=pl.BlockSpec((c_len, R_W), lambda b, c: (b * nc + c, 0)),
        out_shape=jax.ShapeDtypeStruct((batch * seq, R_W), MXU_DTYPE),
        scratch_shapes=[pltpu.VMEM((R_HEADS, R_HEAD_DIM, R_HEAD_DIM), jnp.float32),
                        pltpu.VMEM((R_HEADS, c_len, c_len), jnp.float32),
                        pltpu.VMEM((R_HEADS, 2, c_len, half), jnp.float32)],
        compiler_params=_cparams("arbitrary", "arbitrary"), name="retention")(
            proj, proj, cos, sin, gn_gain.reshape(1, R_W))


def _mixer(h, norm_gain, w_in, q_gain, k_gain, gn_gain, w_out, near_bias, cos, sin, batch, seq):
    head_gains = jnp.stack([q_gain * (A_HEAD_DIM ** -0.5 * LOG2E), k_gain])
    proj, ikw = _project(h, norm_gain, _merged_in_weights(w_in), head_gains)
    a_out = _dsa(proj, ikw, near_bias, batch, seq)
    r_out = _retention(proj, cos, sin, gn_gain, batch, seq)
    return _out_project(a_out, r_out, w_out.astype(MXU_DTYPE), h)


def kernel(x, rel_bias, norm_mix, w_in, q_gain, k_gain, ret_gain, w_out, norm_ffn,
           ffn_w_gate, ffn_w_up, ffn_w_down, moe_router, moe_w_gate, moe_w_up, moe_w_down):
    batch, seq, d = x.shape
    depth = norm_mix.shape[0]
    h = x.reshape(batch * seq, d)
    near_bias = _near_bias(rel_bias)
    cos, sin = _rope_tables(seq)
    for l in range(depth):
        h = _mixer(h, norm_mix[l], w_in[l], q_gain[l], k_gain[l], ret_gain[l], w_out[l],
                   near_bias, cos, sin, batch, seq)
        j = l // 2
        if l % 2 == 0:
            hid = _gate_up(h, norm_ffn[l], ffn_w_gate[j].astype(MXU_DTYPE),
                           ffn_w_up[j].astype(MXU_DTYPE))
            f = ffn_w_down.shape[1]
            h = _down(hid, ffn_w_down[j].astype(MXU_DTYPE), h)
        else:
            h = _moe(h, norm_ffn[l], moe_router[j], moe_w_gate[j], moe_w_up[j], moe_w_down[j])
    return h.reshape(batch, seq, d)
```
